```python
import jax, jax.numpy as jnp
from jax import lax
import numpy as np


D_MODEL = 1024
BATCH = 8
SEQ = 2048
DEPTH = 4

GM_GROUPS = 8
GM_GROUP_DIM = D_MODEL // 16
GM_WIDTH = GM_GROUPS * GM_GROUP_DIM
GM_CHUNK = 128
ATTN_HEADS = 8
HEAD_DIM = D_MODEL // 16
ATTN_WIDTH = ATTN_HEADS * HEAD_DIM
MOBA_BLOCK = 256
MOBA_TOPK = 3
MOBA_QCHUNK = 32
ROPE_THETA = 10000.0
D_FF = -(-8 * D_MODEL // (3 * 256)) * 256
IN_WIDTH = 2 * GM_WIDTH + 3 * ATTN_WIDTH + 2 * D_MODEL
NORM_EPS = 1e-6
NEG_INF = -1e30

kernel_name = 'hybrid_gmlp_moba_block'


def rms_norm(x, g):
    xf = x.astype(jnp.float32)
    y = xf * lax.rsqrt(jnp.mean(xf * xf, axis=-1, keepdims=True) + NORM_EPS)
    return (y * g.astype(jnp.float32)).astype(x.dtype)


def layer_norm(x, g, b):
    xf = x.astype(jnp.float32)
    mu = jnp.mean(xf, axis=-1, keepdims=True)
    var = jnp.mean(jnp.square(xf - mu), axis=-1, keepdims=True)
    y = (xf - mu) * lax.rsqrt(var + NORM_EPS)
    return (y * g.astype(jnp.float32) + b.astype(jnp.float32)).astype(x.dtype)


def rope_tables(positions):
    inv_freq = 1.0 / (ROPE_THETA ** (jnp.arange(0, HEAD_DIM, 2, dtype=jnp.float32) / HEAD_DIM))
    ang = positions.astype(jnp.float32)[..., None] * inv_freq
    return jnp.cos(ang)[:, :, None, :], jnp.sin(ang)[:, :, None, :]


def apply_rope(x, cos, sin):
    xf = x.astype(jnp.float32)
    x1, x2 = jnp.split(xf, 2, axis=-1)
    return jnp.concatenate([x1 * cos - x2 * sin, x2 * cos + x1 * sin], axis=-1).astype(x.dtype)


def spatial_gating_unit(u, v, w_s, b_s, ln_g, ln_b):
    B, S, _ = v.shape
    n_chunks = S // GM_CHUNK
    v = layer_norm(v, ln_g, ln_b)
    vc = v.reshape(B, n_chunks, GM_CHUNK, GM_GROUPS, GM_GROUP_DIM)
    causal = jnp.tril(jnp.ones((GM_CHUNK, GM_CHUNK), dtype=bool))
    w = jnp.where(causal, w_s, jnp.zeros((), w_s.dtype))
    mixed = jnp.einsum('gij,bcjgd->bcigd', w, vc) + b_s.T[None, None, :, :, None]
    return u * mixed.reshape(B, S, GM_WIDTH)


def moba_attention(q, k, v):
    B, H, S, hd = q.shape
    s_pad = -(-S // MOBA_BLOCK) * MOBA_BLOCK
    pad = ((0, 0), (0, 0), (0, s_pad - S), (0, 0))
    q = jnp.pad(q, pad)
    k = jnp.pad(k, pad)
    v = jnp.pad(v, pad)
    n_blocks = s_pad // MOBA_BLOCK
    topk = max(1, min(MOBA_TOPK, n_blocks - 1))
    scale = HEAD_DIM ** -0.5

    kb = k.reshape(B, H, n_blocks, MOBA_BLOCK, hd)
    vb = v.reshape(B, H, n_blocks, MOBA_BLOCK, hd)
    kbar = jnp.mean(kb.astype(jnp.float32), axis=3)
    gate = jnp.einsum('bhsd,bhnd->bhsn', q.astype(jnp.float32), kbar)
    q_block = jnp.arange(s_pad) // MOBA_BLOCK
    fully_past = jnp.arange(n_blocks)[None, :] < q_block[:, None]
    gate = jnp.where(fully_past, gate, -jnp.inf)
    _, idx = lax.top_k(gate, topk)
    valid = idx < q_block[:, None]

    n_q = s_pad // MOBA_QCHUNK
    qc = q.reshape(B, H, n_q, MOBA_QCHUNK, hd).transpose(2, 0, 1, 3, 4)
    idxc = idx.reshape(B, H, n_q, MOBA_QCHUNK, topk).transpose(2, 0, 1, 3, 4)
    validc = valid.reshape(B, H, n_q, MOBA_QCHUNK, topk).transpose(2, 0, 1, 3, 4)
    bi = jnp.arange(B)[:, None, None, None]
    hi = jnp.arange(H)[None, :, None, None]

    def step(args):
        c, qq, ii, vv = args
        kg = kb[bi, hi, ii]
        vg = vb[bi, hi, ii]
        s_sel = jnp.einsum('bhqd,bhqnkd->bhqnk', qq, kg, preferred_element_type=jnp.float32) * scale
        s_sel = jnp.where(vv[..., None], s_sel, NEG_INF).reshape(B, H, MOBA_QCHUNK, topk * MOBA_BLOCK)
        blk = (c * MOBA_QCHUNK) // MOBA_BLOCK
        k_own = lax.dynamic_index_in_dim(kb, blk, axis=2, keepdims=False)
        v_own = lax.dynamic_index_in_dim(vb, blk, axis=2, keepdims=False)
        s_own = jnp.einsum('bhqd,bhkd->bhqk', qq, k_own, preferred_element_type=jnp.float32) * scale
        q_pos = c * MOBA_QCHUNK + jnp.arange(MOBA_QCHUNK)
        k_pos = blk * MOBA_BLOCK + jnp.arange(MOBA_BLOCK)
        s_own = jnp.where(k_pos[None, :] <= q_pos[:, None], s_own, NEG_INF)
        p = jax.nn.softmax(jnp.concatenate([s_sel, s_own], axis=-1), axis=-1).astype(v.dtype)
        p_sel = p[..., :topk * MOBA_BLOCK].reshape(B, H, MOBA_QCHUNK, topk, MOBA_BLOCK)
        p_own = p[..., topk * MOBA_BLOCK:]
        return (jnp.einsum('bhqnk,bhqnkd->bhqd', p_sel, vg)
                + jnp.einsum('bhqk,bhkd->bhqd', p_own, v_own))

    out = lax.map(step, (jnp.arange(n_q), qc, idxc, validc))
    out = out.transpose(1, 2, 0, 3, 4).reshape(B, H, s_pad, hd)
    return out[:, :, :S]


def hybrid_layer(x, cos, sin, w_in, w_s, b_s, ln_v_g, ln_v_b, w_proj_a, w_proj_b, w_out,
                 g_mix_pre, g_mix_post, g_ffn_pre, g_ffn_post, w_gate_up, w_down):
    B, S, _ = x.shape
    h = rms_norm(x, g_mix_pre)
    z = h @ w_in
    o1 = GM_WIDTH
    o2 = 2 * GM_WIDTH
    o3 = o2 + ATTN_WIDTH
    o4 = o3 + ATTN_WIDTH
    o5 = o4 + ATTN_WIDTH
    o6 = o5 + D_MODEL
    u, vg, q, k, va, gate_a, gate_b = jnp.split(z, [o1, o2, o3, o4, o5, o6], axis=-1)
    y_a = spatial_gating_unit(jax.nn.gelu(u), jax.nn.gelu(vg), w_s, b_s, ln_v_g, ln_v_b)
    q = apply_rope(q.reshape(B, S, ATTN_HEADS, HEAD_DIM), cos, sin).transpose(0, 2, 1, 3)
    k = apply_rope(k.reshape(B, S, ATTN_HEADS, HEAD_DIM), cos, sin).transpose(0, 2, 1, 3)
    va = va.reshape(B, S, ATTN_HEADS, HEAD_DIM).transpose(0, 2, 1, 3)
    y_b = moba_attention(q, k, va).transpose(0, 2, 1, 3).reshape(B, S, ATTN_WIDTH)
    merged = jax.nn.sigmoid(gate_a) * (y_a @ w_proj_a) + jax.nn.sigmoid(gate_b) * (y_b @ w_proj_b)
    x = x + rms_norm(merged @ w_out, g_mix_post)
    h = rms_norm(x, g_ffn_pre)
    gate, up = jnp.split(h @ w_gate_up, 2, axis=-1)
    f = (jax.nn.silu(gate) * up) @ w_down
    return x + rms_norm(f, g_ffn_post)


def setup_inputs(seed: int = 0) -> dict:
    key = jax.random.key(seed)
    ks = jax.random.split(key, 16)
    f32 = jnp.float32

    def nrm(k_, shape, fan_in):
        return jax.random.normal(k_, shape, f32) * (fan_in ** -0.5)

    def gain(k_, shape):
        return 1.0 + 0.02 * jax.random.normal(k_, shape, f32)

    x = jax.random.normal(ks[0], (BATCH, SEQ, D_MODEL), f32)
    positions = jnp.broadcast_to(jnp.arange(SEQ, dtype=jnp.int32), (BATCH, SEQ))
    return {
        'x': x,
        'positions': positions,
        'w_in': nrm(ks[1], (DEPTH, D_MODEL, IN_WIDTH), D_MODEL),
        'w_s': nrm(ks[2], (DEPTH, GM_GROUPS, GM_CHUNK, GM_CHUNK), GM_CHUNK),
        'b_s': gain(ks[3], (DEPTH, GM_GROUPS, GM_CHUNK)),
        'ln_v_g': gain(ks[4], (DEPTH, GM_WIDTH)),
        'ln_v_b': 0.02 * jax.random.normal(ks[5], (DEPTH, GM_WIDTH), f32),
        'w_proj_a': nrm(ks[6], (DEPTH, GM_WIDTH, D_MODEL), GM_WIDTH),
        'w_proj_b': nrm(ks[7], (DEPTH, ATTN_WIDTH, D_MODEL), ATTN_WIDTH),
        'w_out': nrm(ks[8], (DEPTH, D_MODEL, D_MODEL), D_MODEL),
        'g_mix_pre': gain(ks[9], (DEPTH, D_MODEL)),
        'g_mix_post': gain(ks[10], (DEPTH, D_MODEL)),
        'g_ffn_pre': gain(ks[11], (DEPTH, D_MODEL)),
        'g_ffn_post': gain(ks[12], (DEPTH, D_MODEL)),
        'w_gate_up': nrm(ks[13], (DEPTH, D_MODEL, 2 * D_FF), D_MODEL),
        'w_down': nrm(ks[14], (DEPTH, D_FF, D_MODEL), D_FF),
    }


def reference(x, positions, w_in, w_s, b_s, ln_v_g, ln_v_b, w_proj_a, w_proj_b, w_out,
              g_mix_pre, g_mix_post, g_ffn_pre, g_ffn_post, w_gate_up, w_down):
    cos, sin = rope_tables(positions)
    for l in range(DEPTH):
        x = hybrid_layer(x, cos, sin, w_in[l], w_s[l], b_s[l], ln_v_g[l], ln_v_b[l],
                         w_proj_a[l], w_proj_b[l], w_out[l], g_mix_pre[l], g_mix_post[l],
                         g_ffn_pre[l], g_ffn_post[l], w_gate_up[l], w_down[l])
    return x
```

```python
import functools

import jax
import jax.numpy as jnp
from jax import lax
from jax.experimental import pallas as pl
from jax.experimental.pallas import tpu as pltpu

D_MODEL = 1024
GM_GROUPS = 8
GM_GROUP_DIM = 64
GM_WIDTH = GM_GROUPS * GM_GROUP_DIM
GM_CHUNK = 128
ATTN_HEADS = 8
HEAD_DIM = 64
ATTN_WIDTH = ATTN_HEADS * HEAD_DIM
MOBA_BLOCK = 256
MOBA_TOPK = 3
ROPE_THETA = 10000.0
D_FF = 2816
IN_WIDTH = 2 * GM_WIDTH + 3 * ATTN_WIDTH + 2 * D_MODEL
NORM_EPS = 1e-6
NEG_INF = -1e30

LANES = 128
HEADS_PER_TILE = LANES // HEAD_DIM
VMEM_LIMIT_BYTES = 56 * 1024 * 1024

F32 = jnp.float32
BF16 = jnp.bfloat16


def _rms(x, g):
    return x * lax.rsqrt(jnp.mean(x * x, axis=-1, keepdims=True) + NORM_EPS) * g


def _const_spec(shape):
    nd = len(shape)
    return pl.BlockSpec(shape, lambda *_: (0,) * nd, pipeline_mode=pl.Buffered(1))


def _in_proj_kernel(x_ref, g_ref, w_ref, cos_ref, sin_ref, lng_ref, lnb_ref,
                    u_ref, v_ref, q_ref, k_ref, va_ref, ga_ref, gb_ref):
    h = _rms(x_ref[...], g_ref[...]).astype(BF16)

    def mm(lo, width):
        return jnp.dot(h, w_ref[:, lo:lo + width], preferred_element_type=F32)

    u_ref[...] = jax.nn.gelu(mm(0, GM_WIDTH)).astype(BF16)

    v = jax.nn.gelu(mm(GM_WIDTH, GM_WIDTH))
    mu = jnp.mean(v, axis=-1, keepdims=True)
    vc = v - mu
    var = jnp.mean(vc * vc, axis=-1, keepdims=True)
    v_ref[...] = (vc * lax.rsqrt(var + NORM_EPS) * lng_ref[...] + lnb_ref[...]).astype(BF16)

    cos = cos_ref[...]
    sin = sin_ref[...]
    first_half = (lax.broadcasted_iota(jnp.int32, cos.shape, 1) % HEAD_DIM) < HEAD_DIM // 2

    def rope(z, out_ref, scale):
        for c in range(ATTN_WIDTH // LANES):
            zc = z[:, c * LANES:(c + 1) * LANES]
            partner = jnp.where(first_half,
                                pltpu.roll(zc, LANES - HEAD_DIM // 2, 1),
                                pltpu.roll(zc, HEAD_DIM // 2, 1))
            r = zc * cos + partner * sin
            if scale != 1.0:
                r = r * scale
            out_ref[:, c * LANES:(c + 1) * LANES] = r.astype(BF16)

    o = 2 * GM_WIDTH
    rope(mm(o, ATTN_WIDTH), q_ref, HEAD_DIM ** -0.5)
    rope(mm(o + ATTN_WIDTH, ATTN_WIDTH), k_ref, 1.0)
    va_ref[...] = mm(o + 2 * ATTN_WIDTH, ATTN_WIDTH).astype(BF16)
    o += 3 * ATTN_WIDTH
    ga_ref[...] = jax.nn.sigmoid(mm(o, D_MODEL)).astype(BF16)
    gb_ref[...] = jax.nn.sigmoid(mm(o + D_MODEL, D_MODEL)).astype(BF16)


def _in_proj(x, g, w, cos, sin, lng, lnb, tm):
    t = x.shape[0]
    row = lambda width: pl.BlockSpec((tm, width), lambda i: (i, 0))
    widths = (GM_WIDTH, GM_WIDTH, ATTN_WIDTH, ATTN_WIDTH, ATTN_WIDTH, D_MODEL, D_MODEL)
    return pl.pallas_call(
        _in_proj_kernel,
        grid=(t // tm,),
        in_specs=[row(D_MODEL), _const_spec((1, D_MODEL)), _const_spec((D_MODEL, IN_WIDTH)),
                  row(LANES), row(LANES), _const_spec((1, GM_WIDTH)), _const_spec((1, GM_WIDTH))],
        out_specs=[row(wd) for wd in widths],
        out_shape=[jax.ShapeDtypeStruct((t, wd), BF16) for wd in widths],
        compiler_params=pltpu.CompilerParams(dimension_semantics=("arbitrary",),
                                             vmem_limit_bytes=VMEM_LIMIT_BYTES),
        name="in_proj",
    )(x, g, w, cos, sin, lng, lnb)


def _moba_kernel(q_ref, k_ref, v_ref, o_ref, kaug_ref, kbar_ref):
    n = pl.program_id(1)
    nb = k_ref.shape[1] // MOBA_BLOCK
    lane = lax.broadcasted_iota(jnp.int32, (1, LANES), 1)

    @pl.when(n == 0)
    def _prepare_keys():
        for hh in range(HEADS_PER_TILE):
            own = (lane // HEAD_DIM) == hh
            other_base = HEAD_DIM * (1 - hh)
            for j in range(nb):
                kb = k_ref[0, j * MOBA_BLOCK:(j + 1) * MOBA_BLOCK, :]
                onehot = jnp.where(lane == other_base + j, 1.0, 0.0).astype(BF16)
                kaug_ref[hh, j * MOBA_BLOCK:(j + 1) * MOBA_BLOCK, :] = jnp.where(own, kb, onehot)
                kbar = jnp.where(own, jnp.mean(kb.astype(F32), axis=0, keepdims=True), 0.0)
                hi = kbar.astype(BF16)
                lo = (kbar - hi.astype(F32)).astype(BF16)
                kbar_ref[hh, j:j + 1, :] = hi.astype(F32)
                kbar_ref[hh, nb + j:nb + j + 1, :] = lo.astype(F32)

    q = q_ref[0]
    row_id = lax.broadcasted_iota(jnp.int32, (MOBA_BLOCK, MOBA_BLOCK), 0)
    col_id = lax.broadcasted_iota(jnp.int32, (MOBA_BLOCK, MOBA_BLOCK), 1)
    causal = col_id <= row_id
    eye = jnp.where(row_id == col_id, 1.0, 0.0).astype(BF16)
    jidx = lax.broadcasted_iota(jnp.int32, (nb, MOBA_BLOCK), 0)
    nt = (((1,), (1,)), ((), ()))

    outs = []
    for hh in range(HEADS_PER_TILE):
        own = (lane // HEAD_DIM) == hh
        other_base = HEAD_DIM * (1 - hh)
        q_own = jnp.where(own, q, jnp.zeros_like(q))

        g2 = lax.dot_general(kbar_ref[hh].astype(BF16), q_own, nt, preferred_element_type=F32)
        gate = jnp.where(jidx < n, g2[:nb] + g2[nb:], -jnp.inf)
        cnt = jnp.zeros((nb, MOBA_BLOCK), jnp.int32)
        for jp in range(nb):
            gj = gate[jp:jp + 1, :]
            ahead = (gj > gate) | ((gj == gate) & (jp < jidx))
            cnt = cnt + jnp.where(ahead, 1, 0)
        keep = ((cnt < MOBA_TOPK) & (jidx < n)) | (jidx == n)
        keep_f = jnp.where(keep, 1.0, 0.0)
        pieces = [keep_f, jnp.zeros((LANES - other_base - nb, MOBA_BLOCK), F32)]
        if other_base:
            pieces.insert(0, jnp.zeros((other_base, MOBA_BLOCK), F32))
        placed = jnp.concatenate(pieces, axis=0).astype(BF16)
        keep_cols = lax.dot_general(eye, placed, nt, preferred_element_type=F32)
        bias = ((keep_cols - 1.0) * (-NEG_INF)).astype(BF16)
        q_aug = jnp.where(own, q, bias)

        def scores(j):
            kb = kaug_ref[hh, pl.ds(pl.multiple_of(j * MOBA_BLOCK, MOBA_BLOCK), MOBA_BLOCK), :]
            return lax.dot_general(q_aug, kb, nt, preferred_element_type=F32)

        def values(j):
            return v_ref[0, pl.ds(pl.multiple_of(j * MOBA_BLOCK, MOBA_BLOCK), MOBA_BLOCK), :]

        s = jnp.where(causal, scores(n), NEG_INF)
        m0 = jnp.max(s, axis=1, keepdims=True)
        p = jnp.exp(s - m0)
        l0 = jnp.sum(p, axis=1, keepdims=True)
        acc0 = jnp.dot(p.astype(BF16), values(n), preferred_element_type=F32)

        def past_block(j, carry):
            m, l, acc = carry
            s = scores(j)
            m_new = jnp.maximum(m, jnp.max(s, axis=1, keepdims=True))
            alpha = jnp.exp(m - m_new)
            p = jnp.exp(s - m_new)
            l = alpha * l + jnp.sum(p, axis=1, keepdims=True)
            acc = alpha * acc + jnp.dot(p.astype(BF16), values(j), preferred_element_type=F32)
            return m_new, l, acc

        _, l, acc = lax.fori_loop(0, n, past_block, (m0, l0, acc0))
        outs.append(acc / l)

    o_ref[0] = jnp.where(lane < HEAD_DIM, outs[0], outs[1]).astype(BF16)


def _moba(q, k, v):
    b, s, w = q.shape
    tiles = w // LANES
    nb = s // MOBA_BLOCK
    qspec = pl.BlockSpec((1, MOBA_BLOCK, LANES), lambda bp, n: (bp // tiles, n, bp % tiles))
    kvspec = pl.BlockSpec((1, s, LANES), lambda bp, n: (bp // tiles, 0, bp % tiles))
    return pl.pallas_call(
        _moba_kernel,
        grid=(b * tiles, nb),
        in_specs=[qspec, kvspec, kvspec],
        out_specs=qspec,
        out_shape=jax.ShapeDtypeStruct((b, s, w), BF16),
        scratch_shapes=[pltpu.VMEM((HEADS_PER_TILE, s, LANES), BF16),
                        pltpu.VMEM((HEADS_PER_TILE, 2 * nb, LANES), F32)],
        compiler_params=pltpu.CompilerParams(dimension_semantics=("arbitrary", "arbitrary"),
                                             vmem_limit_bytes=VMEM_LIMIT_BYTES),
        name="moba",
    )(q, k, v)


def _mix_out_kernel(x_ref, u_ref, v_ref, yb_ref, ga_ref, gb_ref, ws_ref, bmat_ref,
                    wpa_ref, wpb_ref, wo_ref, gpost_ref, o_ref):
    tm = x_ref.shape[0]
    lane = lax.broadcasted_iota(jnp.int32, (1, LANES), 1)
    first = lane < GM_GROUP_DIM
    zero = jnp.zeros((GM_CHUNK, LANES), BF16)
    wi = lax.broadcasted_iota(jnp.int32, (GM_CHUNK, 2 * GM_CHUNK), 0)
    wj = lax.broadcasted_iota(jnp.int32, (GM_CHUNK, 2 * GM_CHUNK), 1) % GM_CHUNK
    tril = wj <= wi
    bmat = bmat_ref[...]

    ya = []
    for c in range(tm // GM_CHUNK):
        rows = slice(c * GM_CHUNK, (c + 1) * GM_CHUNK)
        cols = []
        for gp in range(GM_WIDTH // LANES):
            vp = v_ref[rows, gp * LANES:(gp + 1) * LANES]
            stacked = jnp.concatenate([jnp.where(first, vp, zero), jnp.where(first, zero, vp)], axis=0)
            w = jnp.where(tril, ws_ref[gp], jnp.zeros_like(ws_ref[gp]))
            cols.append(jnp.dot(w, stacked, preferred_element_type=F32))
        mixed = jnp.concatenate(cols, axis=1) + bmat
        ya.append((u_ref[rows, :].astype(F32) * mixed).astype(BF16))
    ya = jnp.concatenate(ya, axis=0)

    pa = jnp.dot(ya, wpa_ref[...], preferred_element_type=F32)
    pb = jnp.dot(yb_ref[...], wpb_ref[...], preferred_element_type=F32)
    merged = (ga_ref[...].astype(F32) * pa + gb_ref[...].astype(F32) * pb).astype(BF16)
    y = jnp.dot(merged, wo_ref[...], preferred_element_type=F32)
    o_ref[...] = x_ref[...] + _rms(y, gpost_ref[...])


def _mix_out(x, u, v, yb, ga, gb, ws, bmat, wpa, wpb, wo, gpost, tm):
    t = x.shape[0]
    row = lambda width: pl.BlockSpec((tm, width), lambda i: (i, 0))
    return pl.pallas_call(
        _mix_out_kernel,
        grid=(t // tm,),
        in_specs=[row(D_MODEL), row(GM_WIDTH), row(GM_WIDTH), row(ATTN_WIDTH), row(D_MODEL), row(D_MODEL),
                  _const_spec(ws.shape), _const_spec(bmat.shape), _const_spec(wpa.shape),
                  _const_spec(wpb.shape), _const_spec(wo.shape), _const_spec((1, D_MODEL))],
        out_specs=row(D_MODEL),
        out_shape=jax.ShapeDtypeStruct((t, D_MODEL), F32),
        compiler_params=pltpu.CompilerParams(dimension_semantics=("arbitrary",),
                                             vmem_limit_bytes=VMEM_LIMIT_BYTES),
        name="mix_out",
    )(x, u, v, yb, ga, gb, ws, bmat, wpa, wpb, wo, gpost)


def _ffn_kernel(x_ref, gpre_ref, wgu_ref, wd_ref, gpost_ref, o_ref):
    x = x_ref[...]
    h = _rms(x, gpre_ref[...]).astype(BF16)
    gu = jnp.dot(h, wgu_ref[...], preferred_element_type=F32)
    act = (jax.nn.silu(gu[:, :D_FF]) * gu[:, D_FF:]).astype(BF16)
    f = jnp.dot(act, wd_ref[...], preferred_element_type=F32)
    o_ref[...] = x + _rms(f, gpost_ref[...])


def _ffn(x, gpre, wgu, wd, gpost, tm):
    t = x.shape[0]
    row = pl.BlockSpec((tm, D_MODEL), lambda i: (i, 0))
    return pl.pallas_call(
        _ffn_kernel,
        grid=(t // tm,),
        in_specs=[row, _const_spec((1, D_MODEL)), _const_spec(wgu.shape), _const_spec(wd.shape),
                  _const_spec((1, D_MODEL))],
        out_specs=row,
        out_shape=jax.ShapeDtypeStruct((t, D_MODEL), F32),
        compiler_params=pltpu.CompilerParams(dimension_semantics=("arbitrary",),
                                             vmem_limit_bytes=VMEM_LIMIT_BYTES),
        name="ffn",
    )(x, gpre, wgu, wd, gpost)


def _rope_tables(positions):
    half = HEAD_DIM // 2
    inv_freq = 1.0 / (ROPE_THETA ** (jnp.arange(0, HEAD_DIM, 2, dtype=F32) / HEAD_DIM))
    ang = positions.astype(F32).reshape(-1, 1) * inv_freq
    cos = jnp.cos(ang)
    sin = jnp.sin(ang)
    cos_t = jnp.tile(cos, (1, LANES // half))
    sin_t = jnp.tile(jnp.concatenate([-sin, sin], axis=1), (1, HEADS_PER_TILE))
    return cos_t, sin_t


def kernel(x, positions, w_in, w_s, b_s, ln_v_g, ln_v_b, w_proj_a, w_proj_b, w_out,
           g_mix_pre, g_mix_post, g_ffn_pre, g_ffn_post, w_gate_up, w_down):
    b, s, d = x.shape
    depth = w_in.shape[0]
    t = b * s
    cos_t, sin_t = _rope_tables(positions)
    xf = x.reshape(t, d)
    for l in range(depth):
        ws = w_s[l].reshape(GM_GROUPS // 2, 2, GM_CHUNK, GM_CHUNK).transpose(0, 2, 1, 3)
        ws = ws.reshape(GM_GROUPS // 2, GM_CHUNK, 2 * GM_CHUNK).astype(BF16)
        bmat = jnp.repeat(b_s[l].T, GM_GROUP_DIM, axis=1)
        u, v, q, k, va, ga, gb = _in_proj(
            xf, g_mix_pre[l].reshape(1, d), w_in[l].astype(BF16), cos_t, sin_t,
            ln_v_g[l].reshape(1, -1), ln_v_b[l].reshape(1, -1), tm=512)
        yb = _moba(q.reshape(b, s, -1), k.reshape(b, s, -1), va.reshape(b, s, -1)).reshape(t, -1)
        xf = _mix_out(xf, u, v, yb, ga, gb, ws, bmat, w_proj_a[l].astype(BF16),
                      w_proj_b[l].astype(BF16), w_out[l].astype(BF16),
                      g_mix_post[l].reshape(1, d), tm=512)
        xf = _ffn(xf, g_ffn_pre[l].reshape(1, d), w_gate_up[l].astype(BF16),
                  w_down[l].astype(BF16), g_ffn_post[l].reshape(1, d), tm=256)
    return xf.reshape(b, s, d)
```

```python
import jax
import jax.numpy as jnp
from jax import lax
from jax.experimental import pallas as pl
from jax.experimental.pallas import tpu as pltpu

D_MODEL = 1024
GM_GROUPS = 8
GM_GROUP_DIM = 64
GM_WIDTH = GM_GROUPS * GM_GROUP_DIM
GM_CHUNK = 128
ATTN_HEADS = 8
HEAD_DIM = 64
ATTN_WIDTH = ATTN_HEADS * HEAD_DIM
MOBA_BLOCK = 256
MOBA_TOPK = 3
ROPE_THETA = 10000.0
D_FF = 2816
IN_WIDTH = 2 * GM_WIDTH + 3 * ATTN_WIDTH + 2 * D_MODEL
NORM_EPS = 1e-6
NEG_INF = -1e30

LANES = 128
HEADS_PER_TILE = LANES // HEAD_DIM
VMEM_LIMIT_BYTES = 56 * 1024 * 1024

F32 = jnp.float32
BF16 = jnp.bfloat16


def _rms(x, g):
    return x * lax.rsqrt(jnp.mean(x * x, axis=-1, keepdims=True) + NORM_EPS) * g


def _const_spec(shape):
    nd = len(shape)
    return pl.BlockSpec(shape, lambda *_: (0,) * nd, pipeline_mode=pl.Buffered(1))


def _in_proj_kernel(x_ref, g_ref, w_ref, cos_ref, sin_ref, lng_ref, lnb_ref,
                    u_ref, v_ref, q_ref, k_ref, va_ref, ga_ref, gb_ref):
    h = _rms(x_ref[...], g_ref[...]).astype(BF16)

    def mm(lo, width):
        return jnp.dot(h, w_ref[:, lo:lo + width], preferred_element_type=F32)

    u_ref[...] = jax.nn.gelu(mm(0, GM_WIDTH)).astype(BF16)

    v = jax.nn.gelu(mm(GM_WIDTH, GM_WIDTH))
    mu = jnp.mean(v, axis=-1, keepdims=True)
    vc = v - mu
    var = jnp.mean(vc * vc, axis=-1, keepdims=True)
    v_ref[...] = (vc * lax.rsqrt(var + NORM_EPS) * lng_ref[...] + lnb_ref[...]).astype(BF16)

    cos = cos_ref[...]
    sin = sin_ref[...]
    first_half = (lax.broadcasted_iota(jnp.int32, cos.shape, 1) % HEAD_DIM) < HEAD_DIM // 2

    def rope(z, out_ref, scale):
        for c in range(ATTN_WIDTH // LANES):
            zc = z[:, c * LANES:(c + 1) * LANES]
            partner = jnp.where(first_half,
                                pltpu.roll(zc, LANES - HEAD_DIM // 2, 1),
                                pltpu.roll(zc, HEAD_DIM // 2, 1))
            r = zc * cos + partner * sin
            if scale != 1.0:
                r = r * scale
            out_ref[:, c * LANES:(c + 1) * LANES] = r.astype(BF16)

    o = 2 * GM_WIDTH
    rope(mm(o, ATTN_WIDTH), q_ref, HEAD_DIM ** -0.5)
    rope(mm(o + ATTN_WIDTH, ATTN_WIDTH), k_ref, 1.0)
    va_ref[...] = mm(o + 2 * ATTN_WIDTH, ATTN_WIDTH).astype(BF16)
    o += 3 * ATTN_WIDTH
    ga_ref[...] = jax.nn.sigmoid(mm(o, D_MODEL)).astype(BF16)
    gb_ref[...] = jax.nn.sigmoid(mm(o + D_MODEL, D_MODEL)).astype(BF16)


def _in_proj(x, g, w, cos, sin, lng, lnb, tm):
    t = x.shape[0]
    row = lambda width: pl.BlockSpec((tm, width), lambda i: (i, 0))
    widths = (GM_WIDTH, GM_WIDTH, ATTN_WIDTH, ATTN_WIDTH, ATTN_WIDTH, D_MODEL, D_MODEL)
    return pl.pallas_call(
        _in_proj_kernel,
        grid=(t // tm,),
        in_specs=[row(D_MODEL), _const_spec((1, D_MODEL)), _const_spec((D_MODEL, IN_WIDTH)),
                  row(LANES), row(LANES), _const_spec((1, GM_WIDTH)), _const_spec((1, GM_WIDTH))],
        out_specs=[row(wd) for wd in widths],
        out_shape=[jax.ShapeDtypeStruct((t, wd), BF16) for wd in widths],
        compiler_params=pltpu.CompilerParams(dimension_semantics=("arbitrary",),
                                             vmem_limit_bytes=VMEM_LIMIT_BYTES),
        name="in_proj",
    )(x, g, w, cos, sin, lng, lnb)


def _moba_kernel(q_ref, k_ref, v_ref, o_ref, kaug_ref, kbar_ref):
    nb = k_ref.shape[1] // MOBA_BLOCK
    lane = lax.broadcasted_iota(jnp.int32, (1, LANES), 1)
    row_id = lax.broadcasted_iota(jnp.int32, (MOBA_BLOCK, MOBA_BLOCK), 0)
    col_id = lax.broadcasted_iota(jnp.int32, (MOBA_BLOCK, MOBA_BLOCK), 1)
    causal = col_id <= row_id
    eye = jnp.where(row_id == col_id, 1.0, 0.0).astype(BF16)
    jidx = lax.broadcasted_iota(jnp.int32, (nb, MOBA_BLOCK), 0)
    nt = (((1,), (1,)), ((), ()))

    for hh in range(HEADS_PER_TILE):
        own = (lane // HEAD_DIM) == hh
        other_base = HEAD_DIM * (1 - hh)
        for j in range(nb):
            blk = slice(j * MOBA_BLOCK, (j + 1) * MOBA_BLOCK)
            kb = k_ref[0, blk, :]
            onehot = jnp.where(lane == other_base + j, 1.0, 0.0).astype(BF16)
            kaug_ref[hh, blk, :] = jnp.where(own, kb, onehot)
            kbar = jnp.where(own, jnp.mean(kb.astype(F32), axis=0, keepdims=True), 0.0)
            hi = kbar.astype(BF16)
            lo = (kbar - hi.astype(F32)).astype(BF16)
            kbar_ref[hh, j:j + 1, :] = hi.astype(F32)
            kbar_ref[hh, nb + j:nb + j + 1, :] = lo.astype(F32)

    for n in range(nb):
        blk = slice(n * MOBA_BLOCK, (n + 1) * MOBA_BLOCK)
        past = n * MOBA_BLOCK
        q = q_ref[0, blk, :]
        outs = []
        for hh in range(HEADS_PER_TILE):
            own = (lane // HEAD_DIM) == hh
            other_base = HEAD_DIM * (1 - hh)
            if n > MOBA_TOPK:
                q_own = jnp.where(own, q, jnp.zeros_like(q))
                g2 = lax.dot_general(kbar_ref[hh].astype(BF16), q_own, nt, preferred_element_type=F32)
                gate = jnp.where(jidx < n, g2[:nb] + g2[nb:], -jnp.inf)
                cnt = jnp.zeros((nb, MOBA_BLOCK), jnp.int32)
                for jp in range(n):
                    gj = gate[jp:jp + 1, :]
                    ahead = (gj > gate) | ((gj == gate) & (jp < jidx))
                    cnt = cnt + jnp.where(ahead, 1, 0)
                keep = ((cnt < MOBA_TOPK) & (jidx < n)) | (jidx == n)
                keep_f = jnp.where(keep, 1.0, 0.0)
                pieces = [keep_f, jnp.zeros((LANES - other_base - nb, MOBA_BLOCK), F32)]
                if other_base:
                    pieces.insert(0, jnp.zeros((other_base, MOBA_BLOCK), F32))
                placed = jnp.concatenate(pieces, axis=0).astype(BF16)
                keep_cols = lax.dot_general(eye, placed, nt, preferred_element_type=F32)
                bias = ((keep_cols - 1.0) * (-NEG_INF)).astype(BF16)
                q_aug = jnp.where(own, q, bias)
            else:
                q_aug = jnp.where(own, q, jnp.zeros_like(q))

            s_own = lax.dot_general(q_aug, kaug_ref[hh, blk, :], nt, preferred_element_type=F32)
            s_own = jnp.where(causal, s_own, NEG_INF)
            m = jnp.max(s_own, axis=1, keepdims=True)
            if n:
                s_past = lax.dot_general(q_aug, kaug_ref[hh, :past, :], nt, preferred_element_type=F32)
                m = jnp.maximum(m, jnp.max(s_past, axis=1, keepdims=True))
                p = jnp.concatenate([jnp.exp(s_past - m), jnp.exp(s_own - m)], axis=1)
            else:
                p = jnp.exp(s_own - m)
            l = jnp.sum(p, axis=1, keepdims=True)
            acc = jnp.dot(p.astype(BF16), v_ref[0, :past + MOBA_BLOCK, :], preferred_element_type=F32)
            outs.append(acc / l)
        o_ref[0, blk, :] = jnp.where(lane < HEAD_DIM, outs[0], outs[1]).astype(BF16)


def _moba(q, k, v):
    b, s, w = q.shape
    tiles = w // LANES
    nb = s // MOBA_BLOCK
    spec = pl.BlockSpec((1, s, LANES), lambda bp: (bp // tiles, 0, bp % tiles))
    return pl.pallas_call(
        _moba_kernel,
        grid=(b * tiles,),
        in_specs=[spec, spec, spec],
        out_specs=spec,
        out_shape=jax.ShapeDtypeStruct((b, s, w), BF16),
        scratch_shapes=[pltpu.VMEM((HEADS_PER_TILE, s, LANES), BF16),
                        pltpu.VMEM((HEADS_PER_TILE, 2 * nb, LANES), F32)],
        compiler_params=pltpu.CompilerParams(dimension_semantics=("arbitrary",),
                                             vmem_limit_bytes=VMEM_LIMIT_BYTES),
        name="moba",
    )(q, k, v)


def _mix_out_kernel(x_ref, u_ref, v_ref, yb_ref, ga_ref, gb_ref, ws_ref, bmat_ref,
                    wpa_ref, wpb_ref, wo_ref, gpost_ref, o_ref):
    tm = x_ref.shape[0]
    lane = lax.broadcasted_iota(jnp.int32, (1, LANES), 1)
    first = lane < GM_GROUP_DIM
    zero = jnp.zeros((GM_CHUNK, LANES), BF16)
    wi = lax.broadcasted_iota(jnp.int32, (GM_CHUNK, 2 * GM_CHUNK), 0)
    wj = lax.broadcasted_iota(jnp.int32, (GM_CHUNK, 2 * GM_CHUNK), 1) % GM_CHUNK
    tril = wj <= wi
    bmat = bmat_ref[...]

    ya = []
    for c in range(tm // GM_CHUNK):
        rows = slice(c * GM_CHUNK, (c + 1) * GM_CHUNK)
        cols = []
        for gp in range(GM_WIDTH // LANES):
            vp = v_ref[rows, gp * LANES:(gp + 1) * LANES]
            stacked = jnp.concatenate([jnp.where(first, vp, zero), jnp.where(first, zero, vp)], axis=0)
            w = jnp.where(tril, ws_ref[gp], jnp.zeros_like(ws_ref[gp]))
            cols.append(jnp.dot(w, stacked, preferred_element_type=F32))
        mixed = jnp.concatenate(cols, axis=1) + bmat
        ya.append((u_ref[rows, :].astype(F32) * mixed).astype(BF16))
    ya = jnp.concatenate(ya, axis=0)

    pa = jnp.dot(ya, wpa_ref[...], preferred_element_type=F32)
    pb = jnp.dot(yb_ref[...], wpb_ref[...], preferred_element_type=F32)
    merged = (ga_ref[...].astype(F32) * pa + gb_ref[...].astype(F32) * pb).astype(BF16)
    y = jnp.dot(merged, wo_ref[...], preferred_element_type=F32)
    o_ref[...] = x_ref[...] + _rms(y, gpost_ref[...])


def _mix_out(x, u, v, yb, ga, gb, ws, bmat, wpa, wpb, wo, gpost, tm):
    t = x.shape[0]
    row = lambda width: pl.BlockSpec((tm, width), lambda i: (i, 0))
    return pl.pallas_call(
        _mix_out_kernel,
        grid=(t // tm,),
        in_specs=[row(D_MODEL), row(GM_WIDTH), row(GM_WIDTH), row(ATTN_WIDTH), row(D_MODEL), row(D_MODEL),
                  _const_spec(ws.shape), _const_spec(bmat.shape), _const_spec(wpa.shape),
                  _const_spec(wpb.shape), _const_spec(wo.shape), _const_spec((1, D_MODEL))],
        out_specs=row(D_MODEL),
        out_shape=jax.ShapeDtypeStruct((t, D_MODEL), F32),
        compiler_params=pltpu.CompilerParams(dimension_semantics=("arbitrary",),
                                             vmem_limit_bytes=VMEM_LIMIT_BYTES),
        name="mix_out",
    )(x, u, v, yb, ga, gb, ws, bmat, wpa, wpb, wo, gpost)


def _ffn_kernel(x_ref, gpre_ref, wgu_ref, wd_ref, gpost_ref, o_ref):
    x = x_ref[...]
    h = _rms(x, gpre_ref[...]).astype(BF16)
    gu = jnp.dot(h, wgu_ref[...], preferred_element_type=F32)
    act = (jax.nn.silu(gu[:, :D_FF]) * gu[:, D_FF:]).astype(BF16)
    f = jnp.dot(act, wd_ref[...], preferred_element_type=F32)
    o_ref[...] = x + _rms(f, gpost_ref[...])


def _ffn(x, gpre, wgu, wd, gpost, tm):
    t = x.shape[0]
    row = pl.BlockSpec((tm, D_MODEL), lambda i: (i, 0))
    return pl.pallas_call(
        _ffn_kernel,
        grid=(t // tm,),
        in_specs=[row, _const_spec((1, D_MODEL)), _const_spec(wgu.shape), _const_spec(wd.shape),
                  _const_spec((1, D_MODEL))],
        out_specs=row,
        out_shape=jax.ShapeDtypeStruct((t, D_MODEL), F32),
        compiler_params=pltpu.CompilerParams(dimension_semantics=("arbitrary",),
                                             vmem_limit_bytes=VMEM_LIMIT_BYTES),
        name="ffn",
    )(x, gpre, wgu, wd, gpost)


def _rope_tables(positions):
    half = HEAD_DIM // 2
    inv_freq = 1.0 / (ROPE_THETA ** (jnp.arange(0, HEAD_DIM, 2, dtype=F32) / HEAD_DIM))
    ang = positions.astype(F32).reshape(-1, 1) * inv_freq
    cos = jnp.cos(ang)
    sin = jnp.sin(ang)
    cos_t = jnp.tile(cos, (1, LANES // half))
    sin_t = jnp.tile(jnp.concatenate([-sin, sin], axis=1), (1, HEADS_PER_TILE))
    return cos_t, sin_t


def kernel(x, positions, w_in, w_s, b_s, ln_v_g, ln_v_b, w_proj_a, w_proj_b, w_out,
           g_mix_pre, g_mix_post, g_ffn_pre, g_ffn_post, w_gate_up, w_down):
    b, s, d = x.shape
    depth = w_in.shape[0]
    t = b * s
    cos_t, sin_t = _rope_tables(positions)
    xf = x.reshape(t, d)
    for l in range(depth):
        ws = w_s[l].reshape(GM_GROUPS // 2, 2, GM_CHUNK, GM_CHUNK).transpose(0, 2, 1, 3)
        ws = ws.reshape(GM_GROUPS // 2, GM_CHUNK, 2 * GM_CHUNK).astype(BF16)
        bmat = jnp.repeat(b_s[l].T, GM_GROUP_DIM, axis=1)
        u, v, q, k, va, ga, gb = _in_proj(
            xf, g_mix_pre[l].reshape(1, d), w_in[l].astype(BF16), cos_t, sin_t,
            ln_v_g[l].reshape(1, -1), ln_v_b[l].reshape(1, -1), tm=512)
        yb = _moba(q.reshape(b, s, -1), k.reshape(b, s, -1), va.reshape(b, s, -1)).reshape(t, -1)
        xf = _mix_out(xf, u, v, yb, ga, gb, ws, bmat, w_proj_a[l].astype(BF16),
                      w_proj_b[l].astype(BF16), w_out[l].astype(BF16),
                      g_mix_post[l].reshape(1, d), tm=512)
        xf = _ffn(xf, g_ffn_pre[l].reshape(1, d), w_gate_up[l].astype(BF16),
                  w_down[l].astype(BF16), g_ffn_post[l].reshape(1, d), tm=256)
    return xf.reshape(b, s, d)
```

```python
import jax
import jax.numpy as jnp
from jax import lax
from jax.experimental import pallas as pl
from jax.experimental.pallas import tpu as pltpu

D_MODEL = 1024
GM_GROUPS = 8
GM_GROUP_DIM = 64
GM_WIDTH = GM_GROUPS * GM_GROUP_DIM
GM_CHUNK = 128
ATTN_HEADS = 8
HEAD_DIM = 64
ATTN_WIDTH = ATTN_HEADS * HEAD_DIM
MOBA_BLOCK = 256
MOBA_TOPK = 3
ROPE_THETA = 10000.0
D_FF = 2816
IN_WIDTH = 2 * GM_WIDTH + 3 * ATTN_WIDTH + 2 * D_MODEL
NORM_EPS = 1e-6
NEG_INF = -1e30

LANES = 128
HEADS_PER_TILE = LANES // HEAD_DIM
VMEM_LIMIT_BYTES = 56 * 1024 * 1024

F32 = jnp.float32
BF16 = jnp.bfloat16


def _rms(x, g):
    return x * lax.rsqrt(jnp.mean(x * x, axis=-1, keepdims=True) + NORM_EPS) * g


def _const_spec(shape):
    nd = len(shape)
    return pl.BlockSpec(shape, lambda *_: (0,) * nd, pipeline_mode=pl.Buffered(1))


def _in_proj_kernel(x_ref, g_ref, w_ref, cos_ref, sin_ref, lng_ref, lnb_ref,
                    u_ref, v_ref, q_ref, k_ref, va_ref, ga_ref, gb_ref):
    h = _rms(x_ref[...], g_ref[...]).astype(BF16)

    def mm(lo, width):
        return jnp.dot(h, w_ref[:, lo:lo + width], preferred_element_type=F32)

    u_ref[...] = jax.nn.gelu(mm(0, GM_WIDTH)).astype(BF16)

    v = jax.nn.gelu(mm(GM_WIDTH, GM_WIDTH))
    mu = jnp.mean(v, axis=-1, keepdims=True)
    vc = v - mu
    var = jnp.mean(vc * vc, axis=-1, keepdims=True)
    v_ref[...] = (vc * lax.rsqrt(var + NORM_EPS) * lng_ref[...] + lnb_ref[...]).astype(BF16)

    cos = cos_ref[...]
    sin = sin_ref[...]
    first_half = (lax.broadcasted_iota(jnp.int32, cos.shape, 1) % HEAD_DIM) < HEAD_DIM // 2

    def rope(z, out_ref, scale):
        for c in range(ATTN_WIDTH // LANES):
            zc = z[:, c * LANES:(c + 1) * LANES]
            partner = jnp.where(first_half,
                                pltpu.roll(zc, LANES - HEAD_DIM // 2, 1),
                                pltpu.roll(zc, HEAD_DIM // 2, 1))
            r = zc * cos + partner * sin
            if scale != 1.0:
                r = r * scale
            out_ref[:, c * LANES:(c + 1) * LANES] = r.astype(BF16)

    o = 2 * GM_WIDTH
    rope(mm(o, ATTN_WIDTH), q_ref, HEAD_DIM ** -0.5)
    rope(mm(o + ATTN_WIDTH, ATTN_WIDTH), k_ref, 1.0)
    va_ref[...] = mm(o + 2 * ATTN_WIDTH, ATTN_WIDTH).astype(BF16)
    o += 3 * ATTN_WIDTH
    ga_ref[...] = jax.nn.sigmoid(mm(o, D_MODEL)).astype(BF16)
    gb_ref[...] = jax.nn.sigmoid(mm(o + D_MODEL, D_MODEL)).astype(BF16)


def _in_proj(x, g, w, cos, sin, lng, lnb, tm):
    t = x.shape[0]
    row = lambda width: pl.BlockSpec((tm, width), lambda i: (i, 0))
    widths = (GM_WIDTH, GM_WIDTH, ATTN_WIDTH, ATTN_WIDTH, ATTN_WIDTH, D_MODEL, D_MODEL)
    return pl.pallas_call(
        _in_proj_kernel,
        grid=(t // tm,),
        in_specs=[row(D_MODEL), _const_spec((1, D_MODEL)), _const_spec((D_MODEL, IN_WIDTH)),
                  row(LANES), row(LANES), _const_spec((1, GM_WIDTH)), _const_spec((1, GM_WIDTH))],
        out_specs=[row(wd) for wd in widths],
        out_shape=[jax.ShapeDtypeStruct((t, wd), BF16) for wd in widths],
        compiler_params=pltpu.CompilerParams(dimension_semantics=("arbitrary",),
                                             vmem_limit_bytes=VMEM_LIMIT_BYTES),
        name="in_proj",
    )(x, g, w, cos, sin, lng, lnb)


def _moba_kernel(q_ref, k_ref, v_ref, o_ref, vt_ref, kbar_ref):
    nb = k_ref.shape[1] // MOBA_BLOCK
    lane = lax.broadcasted_iota(jnp.int32, (1, LANES), 1)
    key_id = lax.broadcasted_iota(jnp.int32, (MOBA_BLOCK, MOBA_BLOCK), 0)
    qry_id = lax.broadcasted_iota(jnp.int32, (MOBA_BLOCK, MOBA_BLOCK), 1)
    causal = key_id <= qry_id
    jidx = lax.broadcasted_iota(jnp.int32, (nb, MOBA_BLOCK), 0)
    drow = lax.broadcasted_iota(jnp.int32, (LANES, 1), 0)
    nt = (((1,), (1,)), ((), ()))

    vt_ref[...] = v_ref[0].astype(F32).T.astype(BF16)
    for j in range(nb):
        kb = k_ref[0, j * MOBA_BLOCK:(j + 1) * MOBA_BLOCK, :]
        kbar = jnp.mean(kb.astype(F32), axis=0, keepdims=True)
        hi = kbar.astype(BF16)
        lo = (kbar - hi.astype(F32)).astype(BF16)
        kbar_ref[j:j + 1, :] = hi.astype(F32)
        kbar_ref[nb + j:nb + j + 1, :] = lo.astype(F32)
    kbar2 = kbar_ref[...].astype(BF16)

    def scores(n, hh):
        past = n * MOBA_BLOCK
        q = q_ref[0, past:past + MOBA_BLOCK, :]
        own = (lane // HEAD_DIM) == hh
        q_own = jnp.where(own, q, jnp.zeros_like(q))
        if n > MOBA_TOPK:
            g2 = lax.dot_general(kbar2, q_own, nt, preferred_element_type=F32)
            gate = jnp.where(jidx < n, g2[:nb] + g2[nb:], -jnp.inf)
            cnt = jnp.zeros((nb, MOBA_BLOCK), jnp.int32)
            for jp in range(n):
                gj = gate[jp:jp + 1, :]
                ahead = (gj > gate) | ((gj == gate) & (jp < jidx))
                cnt = cnt + jnp.where(ahead, 1, 0)
            bias = jnp.where(cnt < MOBA_TOPK, 0.0, NEG_INF)
        else:
            bias = None
        st = lax.dot_general(k_ref[0, :past + MOBA_BLOCK, :], q_own, nt,
                             preferred_element_type=F32)
        s_blocks = [st[j * MOBA_BLOCK:(j + 1) * MOBA_BLOCK] for j in range(n)]
        s_blocks.append(jnp.where(causal, st[past:], NEG_INF))
        m = None
        for j, sj in enumerate(s_blocks):
            mj = jnp.max(sj, axis=0, keepdims=True)
            if bias is not None and j < n:
                mj = mj + bias[j:j + 1, :]
            m = mj if m is None else jnp.maximum(m, mj)
        return s_blocks, m, bias

    def probs(n, s_blocks, m, bias):
        l = None
        p_blocks = []
        for j, sj in enumerate(s_blocks):
            shift = m if (bias is None or j == n) else m - bias[j:j + 1, :]
            pj = jnp.exp(sj - shift)
            lj = jnp.sum(pj, axis=0, keepdims=True)
            l = lj if l is None else l + lj
            p_blocks.append(pj.astype(BF16))
        pt = p_blocks[0] if n == 0 else jnp.concatenate(p_blocks, axis=0)
        return pt, l

    def output(n, pt, l):
        acc = jnp.dot(vt_ref[:, :(n + 1) * MOBA_BLOCK], pt, preferred_element_type=F32)
        return acc / l

    items = [(n, hh) for n in range(nb) for hh in range(HEADS_PER_TILE)]
    stage_a, stage_b, outs = {}, {}, {}
    for i in range(len(items) + 2):
        if i < len(items):
            stage_a[i] = scores(*items[i])
        if 0 <= i - 1 < len(items):
            stage_b[i - 1] = probs(items[i - 1][0], *stage_a.pop(i - 1))
        if 0 <= i - 2 < len(items):
            n, hh = items[i - 2]
            outs[hh] = output(n, *stage_b.pop(i - 2))
            if hh == HEADS_PER_TILE - 1:
                out_t = jnp.where(drow < HEAD_DIM, outs[0], outs[1])
                o_ref[0, n * MOBA_BLOCK:(n + 1) * MOBA_BLOCK, :] = out_t.T.astype(BF16)


def _moba(q, k, v):
    b, s, w = q.shape
    tiles = w // LANES
    nb = s // MOBA_BLOCK
    spec = pl.BlockSpec((1, s, LANES), lambda bp: (bp // tiles, 0, bp % tiles))
    return pl.pallas_call(
        _moba_kernel,
        grid=(b * tiles,),
        in_specs=[spec, spec, spec],
        out_specs=spec,
        out_shape=jax.ShapeDtypeStruct((b, s, w), BF16),
        scratch_shapes=[pltpu.VMEM((LANES, s), BF16),
                        pltpu.VMEM((2 * nb, LANES), F32)],
        compiler_params=pltpu.CompilerParams(dimension_semantics=("arbitrary",),
                                             vmem_limit_bytes=VMEM_LIMIT_BYTES),
        name="moba",
    )(q, k, v)


def _mix_out_kernel(x_ref, u_ref, v_ref, yb_ref, ga_ref, gb_ref, ws_ref, bmat_ref,
                    wpa_ref, wpb_ref, wo_ref, gpost_ref, o_ref):
    tm = x_ref.shape[0]
    lane = lax.broadcasted_iota(jnp.int32, (1, LANES), 1)
    first = lane < GM_GROUP_DIM
    zero = jnp.zeros((GM_CHUNK, LANES), BF16)
    wi = lax.broadcasted_iota(jnp.int32, (GM_CHUNK, 2 * GM_CHUNK), 0)
    wj = lax.broadcasted_iota(jnp.int32, (GM_CHUNK, 2 * GM_CHUNK), 1) % GM_CHUNK
    tril = wj <= wi
    bmat = bmat_ref[...]

    ya = []
    for c in range(tm // GM_CHUNK):
        rows = slice(c * GM_CHUNK, (c + 1) * GM_CHUNK)
        cols = []
        for gp in range(GM_WIDTH // LANES):
            vp = v_ref[rows, gp * LANES:(gp + 1) * LANES]
            stacked = jnp.concatenate([jnp.where(first, vp, zero), jnp.where(first, zero, vp)], axis=0)
            w = jnp.where(tril, ws_ref[gp], jnp.zeros_like(ws_ref[gp]))
            cols.append(jnp.dot(w, stacked, preferred_element_type=F32))
        mixed = jnp.concatenate(cols, axis=1) + bmat
        ya.append((u_ref[rows, :].astype(F32) * mixed).astype(BF16))
    ya = jnp.concatenate(ya, axis=0)

    pa = jnp.dot(ya, wpa_ref[...], preferred_element_type=F32)
    pb = jnp.dot(yb_ref[...], wpb_ref[...], preferred_element_type=F32)
    merged = (ga_ref[...].astype(F32) * pa + gb_ref[...].astype(F32) * pb).astype(BF16)
    y = jnp.dot(merged, wo_ref[...], preferred_element_type=F32)
    o_ref[...] = x_ref[...] + _rms(y, gpost_ref[...])


def _mix_out(x, u, v, yb, ga, gb, ws, bmat, wpa, wpb, wo, gpost, tm):
    t = x.shape[0]
    row = lambda width: pl.BlockSpec((tm, width), lambda i: (i, 0))
    return pl.pallas_call(
        _mix_out_kernel,
        grid=(t // tm,),
        in_specs=[row(D_MODEL), row(GM_WIDTH), row(GM_WIDTH), row(ATTN_WIDTH), row(D_MODEL), row(D_MODEL),
                  _const_spec(ws.shape), _const_spec(bmat.shape), _const_spec(wpa.shape),
                  _const_spec(wpb.shape), _const_spec(wo.shape), _const_spec((1, D_MODEL))],
        out_specs=row(D_MODEL),
        out_shape=jax.ShapeDtypeStruct((t, D_MODEL), F32),
        compiler_params=pltpu.CompilerParams(dimension_semantics=("arbitrary",),
                                             vmem_limit_bytes=VMEM_LIMIT_BYTES),
        name="mix_out",
    )(x, u, v, yb, ga, gb, ws, bmat, wpa, wpb, wo, gpost)


def _ffn_kernel(x_ref, gpre_ref, wgu_ref, wd_ref, gpost_ref, o_ref):
    x = x_ref[...]
    h = _rms(x, gpre_ref[...]).astype(BF16)
    gu = jnp.dot(h, wgu_ref[...], preferred_element_type=F32)
    act = (jax.nn.silu(gu[:, :D_FF]) * gu[:, D_FF:]).astype(BF16)
    f = jnp.dot(act, wd_ref[...], preferred_element_type=F32)
    o_ref[...] = x + _rms(f, gpost_ref[...])


def _ffn(x, gpre, wgu, wd, gpost, tm):
    t = x.shape[0]
    row = pl.BlockSpec((tm, D_MODEL), lambda i: (i, 0))
    return pl.pallas_call(
        _ffn_kernel,
        grid=(t // tm,),
        in_specs=[row, _const_spec((1, D_MODEL)), _const_spec(wgu.shape), _const_spec(wd.shape),
                  _const_spec((1, D_MODEL))],
        out_specs=row,
        out_shape=jax.ShapeDtypeStruct((t, D_MODEL), F32),
        compiler_params=pltpu.CompilerParams(dimension_semantics=("arbitrary",),
                                             vmem_limit_bytes=VMEM_LIMIT_BYTES),
        name="ffn",
    )(x, gpre, wgu, wd, gpost)


def _rope_tables(positions):
    half = HEAD_DIM // 2
    inv_freq = 1.0 / (ROPE_THETA ** (jnp.arange(0, HEAD_DIM, 2, dtype=F32) / HEAD_DIM))
    ang = positions.astype(F32).reshape(-1, 1) * inv_freq
    cos = jnp.cos(ang)
    sin = jnp.sin(ang)
    cos_t = jnp.tile(cos, (1, LANES // half))
    sin_t = jnp.tile(jnp.concatenate([-sin, sin], axis=1), (1, HEADS_PER_TILE))
    return cos_t, sin_t


def kernel(x, positions, w_in, w_s, b_s, ln_v_g, ln_v_b, w_proj_a, w_proj_b, w_out,
           g_mix_pre, g_mix_post, g_ffn_pre, g_ffn_post, w_gate_up, w_down):
    b, s, d = x.shape
    depth = w_in.shape[0]
    t = b * s
    cos_t, sin_t = _rope_tables(positions)
    xf = x.reshape(t, d)
    for l in range(depth):
        ws = w_s[l].reshape(GM_GROUPS // 2, 2, GM_CHUNK, GM_CHUNK).transpose(0, 2, 1, 3)
        ws = ws.reshape(GM_GROUPS // 2, GM_CHUNK, 2 * GM_CHUNK).astype(BF16)
        bmat = jnp.repeat(b_s[l].T, GM_GROUP_DIM, axis=1)
        u, v, q, k, va, ga, gb = _in_proj(
            xf, g_mix_pre[l].reshape(1, d), w_in[l].astype(BF16), cos_t, sin_t,
            ln_v_g[l].reshape(1, -1), ln_v_b[l].reshape(1, -1), tm=512)
        yb = _moba(q.reshape(b, s, -1), k.reshape(b, s, -1), va.reshape(b, s, -1)).reshape(t, -1)
        xf = _mix_out(xf, u, v, yb, ga, gb, ws, bmat, w_proj_a[l].astype(BF16),
                      w_proj_b[l].astype(BF16), w_out[l].astype(BF16),
                      g_mix_post[l].reshape(1, d), tm=512)
        xf = _ffn(xf, g_ffn_pre[l].reshape(1, d), w_gate_up[l].astype(BF16),
                  w_down[l].astype(BF16), g_ffn_post[l].reshape(1, d), tm=256)
    return xf.reshape(b, s, d)
```

```python
import jax
import jax.numpy as jnp
from jax import lax
from jax.experimental import pallas as pl
from jax.experimental.pallas import tpu as pltpu

D_MODEL = 1024
GM_GROUPS = 8
GM_GROUP_DIM = 64
GM_WIDTH = GM_GROUPS * GM_GROUP_DIM
GM_CHUNK = 128
ATTN_HEADS = 8
HEAD_DIM = 64
ATTN_WIDTH = ATTN_HEADS * HEAD_DIM
MOBA_BLOCK = 256
MOBA_TOPK = 3
ROPE_THETA = 10000.0
D_FF = 2816
IN_WIDTH = 2 * GM_WIDTH + 3 * ATTN_WIDTH + 2 * D_MODEL
NORM_EPS = 1e-6
NEG_INF = -1e30

LANES = 128
HEADS_PER_TILE = LANES // HEAD_DIM
VMEM_LIMIT_BYTES = 56 * 1024 * 1024
MXU_TILE = 256

IN_PROJ_SUB = 256
FFN_SUB = 256
MIX_SUB = 256
FFN_CHUNKS = ((0, 5 * MXU_TILE), (5 * MXU_TILE, D_FF - 5 * MXU_TILE))

F32 = jnp.float32
BF16 = jnp.bfloat16


def _rms(x, g):
    return x * lax.rsqrt(jnp.mean(x * x, axis=-1, keepdims=True) + NORM_EPS) * g


def _const_spec(shape):
    nd = len(shape)
    return pl.BlockSpec(shape, lambda *_: (0,) * nd, pipeline_mode=pl.Buffered(1))


def _in_proj_kernel(x_ref, g_ref, w_ref, cos_ref, sin_ref, lng_ref, lnb_ref,
                    u_ref, v_ref, q_ref, k_ref, va_ref, ga_ref, gb_ref):
    tm = x_ref.shape[0]
    first_half = (lax.broadcasted_iota(jnp.int32, (IN_PROJ_SUB, LANES), 1) % HEAD_DIM) < HEAD_DIM // 2

    def gelu_out(out_ref):
        def ep(z, rows):
            out_ref[rows, :] = jax.nn.gelu(z).astype(BF16)
        return ep

    def gelu_ln_out(z, rows):
        v = jax.nn.gelu(z)
        mu = jnp.mean(v, axis=-1, keepdims=True)
        vc = v - mu
        var = jnp.mean(vc * vc, axis=-1, keepdims=True)
        v_ref[rows, :] = (vc * lax.rsqrt(var + NORM_EPS) * lng_ref[...] + lnb_ref[...]).astype(BF16)

    def rope_out(out_ref, scale):
        def ep(z, rows):
            cos = cos_ref[rows, :]
            sin = sin_ref[rows, :]
            for c in range(ATTN_WIDTH // LANES):
                zc = z[:, c * LANES:(c + 1) * LANES]
                partner = jnp.where(first_half,
                                    pltpu.roll(zc, LANES - HEAD_DIM // 2, 1),
                                    pltpu.roll(zc, HEAD_DIM // 2, 1))
                r = zc * cos + partner * sin
                if scale != 1.0:
                    r = r * scale
                out_ref[rows, c * LANES:(c + 1) * LANES] = r.astype(BF16)
        return ep

    def cast_out(z, rows):
        va_ref[rows, :] = z.astype(BF16)

    def sigmoid_out(out_ref):
        def ep(z, rows):
            out_ref[rows, :] = jax.nn.sigmoid(z).astype(BF16)
        return ep

    o_q = 2 * GM_WIDTH
    o_g = o_q + 3 * ATTN_WIDTH
    sections = [
        (0, GM_WIDTH, gelu_out(u_ref)),
        (o_g, D_MODEL, sigmoid_out(ga_ref)),
        (GM_WIDTH, GM_WIDTH, gelu_ln_out),
        (o_g + D_MODEL, D_MODEL, sigmoid_out(gb_ref)),
        (o_q, ATTN_WIDTH, rope_out(q_ref, HEAD_DIM ** -0.5)),
        (o_q + ATTN_WIDTH, ATTN_WIDTH, rope_out(k_ref, 1.0)),
        (o_q + 2 * ATTN_WIDTH, ATTN_WIDTH, cast_out),
    ]
    items = [(r, sec) for r in range(tm // IN_PROJ_SUB) for sec in sections]

    h = {}
    pending = None
    for r, (lo, width, ep) in items:
        rows = slice(r * IN_PROJ_SUB, (r + 1) * IN_PROJ_SUB)
        if r not in h:
            h[r] = _rms(x_ref[rows, :], g_ref[...]).astype(BF16)
        z = jnp.dot(h[r], w_ref[:, lo:lo + width], preferred_element_type=F32)
        if pending is not None:
            pending[0](pending[1], pending[2])
        pending = (ep, z, rows)
    pending[0](pending[1], pending[2])


def _in_proj(x, g, w, cos, sin, lng, lnb, tm):
    t = x.shape[0]
    row = lambda width: pl.BlockSpec((tm, width), lambda i: (i, 0))
    widths = (GM_WIDTH, GM_WIDTH, ATTN_WIDTH, ATTN_WIDTH, ATTN_WIDTH, D_MODEL, D_MODEL)
    return pl.pallas_call(
        _in_proj_kernel,
        grid=(t // tm,),
        in_specs=[row(D_MODEL), _const_spec((1, D_MODEL)), _const_spec((D_MODEL, IN_WIDTH)),
                  row(LANES), row(LANES), _const_spec((1, GM_WIDTH)), _const_spec((1, GM_WIDTH))],
        out_specs=[row(wd) for wd in widths],
        out_shape=[jax.ShapeDtypeStruct((t, wd), BF16) for wd in widths],
        compiler_params=pltpu.CompilerParams(dimension_semantics=("arbitrary",),
                                             vmem_limit_bytes=VMEM_LIMIT_BYTES),
        name="in_proj",
    )(x, g, w, cos, sin, lng, lnb)


def _moba_kernel(q_ref, k_ref, v_ref, o_ref, vt_ref, kbar_ref):
    nb = k_ref.shape[1] // MOBA_BLOCK
    lane = lax.broadcasted_iota(jnp.int32, (1, LANES), 1)
    key_id = lax.broadcasted_iota(jnp.int32, (MOBA_BLOCK, MOBA_BLOCK), 0)
    qry_id = lax.broadcasted_iota(jnp.int32, (MOBA_BLOCK, MOBA_BLOCK), 1)
    causal = key_id <= qry_id
    jidx = lax.broadcasted_iota(jnp.int32, (nb, MOBA_BLOCK), 0)
    drow = lax.broadcasted_iota(jnp.int32, (LANES, 1), 0)
    nt = (((1,), (1,)), ((), ()))

    vt_ref[...] = v_ref[0].astype(F32).T.astype(BF16)
    for j in range(nb):
        kb = k_ref[0, j * MOBA_BLOCK:(j + 1) * MOBA_BLOCK, :]
        kbar = jnp.mean(kb.astype(F32), axis=0, keepdims=True)
        hi = kbar.astype(BF16)
        lo = (kbar - hi.astype(F32)).astype(BF16)
        kbar_ref[j:j + 1, :] = hi.astype(F32)
        kbar_ref[nb + j:nb + j + 1, :] = lo.astype(F32)
    kbar2 = kbar_ref[...].astype(BF16)

    def scores(n, hh):
        past = n * MOBA_BLOCK
        q = q_ref[0, past:past + MOBA_BLOCK, :]
        own = (lane // HEAD_DIM) == hh
        q_own = jnp.where(own, q, jnp.zeros_like(q))
        if n > MOBA_TOPK:
            g2 = lax.dot_general(kbar2, q_own, nt, preferred_element_type=F32)
            gate = jnp.where(jidx < n, g2[:nb] + g2[nb:], -jnp.inf)
            cnt = jnp.zeros((nb, MOBA_BLOCK), jnp.int32)
            for jp in range(n):
                gj = gate[jp:jp + 1, :]
                ahead = (gj > gate) | ((gj == gate) & (jp < jidx))
                cnt = cnt + jnp.where(ahead, 1, 0)
            bias = jnp.where(cnt < MOBA_TOPK, 0.0, NEG_INF)
        else:
            bias = None
        st = lax.dot_general(k_ref[0, :past + MOBA_BLOCK, :], q_own, nt,
                             preferred_element_type=F32)
        s_blocks = [st[j * MOBA_BLOCK:(j + 1) * MOBA_BLOCK] for j in range(n)]
        s_blocks.append(jnp.where(causal, st[past:], NEG_INF))
        m = None
        for j, sj in enumerate(s_blocks):
            mj = jnp.max(sj, axis=0, keepdims=True)
            if bias is not None and j < n:
                mj = mj + bias[j:j + 1, :]
            m = mj if m is None else jnp.maximum(m, mj)
        return s_blocks, m, bias

    def probs(n, s_blocks, m, bias):
        l = None
        p_blocks = []
        for j, sj in enumerate(s_blocks):
            shift = m if (bias is None or j == n) else m - bias[j:j + 1, :]
            pj = jnp.exp(sj - shift)
            lj = jnp.sum(pj, axis=0, keepdims=True)
            l = lj if l is None else l + lj
            p_blocks.append(pj.astype(BF16))
        pt = p_blocks[0] if n == 0 else jnp.concatenate(p_blocks, axis=0)
        return pt, l

    def output(n, pt, l):
        acc = jnp.dot(vt_ref[:, :(n + 1) * MOBA_BLOCK], pt, preferred_element_type=F32)
        return acc / l

    items = [(n, hh) for n in range(nb) for hh in range(HEADS_PER_TILE)]
    stage_a, stage_b, outs = {}, {}, {}
    for i in range(len(items) + 2):
        if i < len(items):
            stage_a[i] = scores(*items[i])
        if 0 <= i - 1 < len(items):
            stage_b[i - 1] = probs(items[i - 1][0], *stage_a.pop(i - 1))
        if 0 <= i - 2 < len(items):
            n, hh = items[i - 2]
            outs[hh] = output(n, *stage_b.pop(i - 2))
            if hh == HEADS_PER_TILE - 1:
                out_t = jnp.where(drow < HEAD_DIM, outs[0], outs[1])
                o_ref[0, n * MOBA_BLOCK:(n + 1) * MOBA_BLOCK, :] = out_t.T.astype(BF16)


def _moba(q, k, v):
    b, s, w = q.shape
    tiles = w // LANES
    nb = s // MOBA_BLOCK
    spec = pl.BlockSpec((1, s, LANES), lambda bp: (bp // tiles, 0, bp % tiles))
    return pl.pallas_call(
        _moba_kernel,
        grid=(b * tiles,),
        in_specs=[spec, spec, spec],
        out_specs=spec,
        out_shape=jax.ShapeDtypeStruct((b, s, w), BF16),
        scratch_shapes=[pltpu.VMEM((LANES, s), BF16),
                        pltpu.VMEM((2 * nb, LANES), F32)],
        compiler_params=pltpu.CompilerParams(dimension_semantics=("arbitrary",),
                                             vmem_limit_bytes=VMEM_LIMIT_BYTES),
        name="moba",
    )(q, k, v)


def _mix_out_kernel(x_ref, u_ref, v_ref, yb_ref, ga_ref, gb_ref, ws_ref, bmat_ref,
                    wpa_ref, wpb_ref, wo_ref, gpost_ref, o_ref):
    tm = x_ref.shape[0]
    lane = lax.broadcasted_iota(jnp.int32, (1, LANES), 1)
    first = lane < GM_GROUP_DIM
    zero = jnp.zeros((GM_CHUNK, LANES), BF16)
    wi = lax.broadcasted_iota(jnp.int32, (GM_CHUNK, 2 * GM_CHUNK), 0)
    wj = lax.broadcasted_iota(jnp.int32, (GM_CHUNK, 2 * GM_CHUNK), 1) % GM_CHUNK
    tril = wj <= wi
    bmat = bmat_ref[...]
    w_tril = [jnp.where(tril, ws_ref[gp], jnp.zeros_like(ws_ref[gp])) for gp in range(GM_WIDTH // LANES)]

    def gating(r):
        n_chunks = MIX_SUB // GM_CHUNK
        chunk_rows = [slice(r * MIX_SUB + c * GM_CHUNK, r * MIX_SUB + (c + 1) * GM_CHUNK)
                      for c in range(n_chunks)]
        cols = []
        for gp in range(GM_WIDTH // LANES):
            stacked = []
            for rows in chunk_rows:
                vp = v_ref[rows, gp * LANES:(gp + 1) * LANES]
                stacked.append(jnp.concatenate([jnp.where(first, vp, zero), jnp.where(first, zero, vp)],
                                               axis=0))
            cols.append(jnp.dot(w_tril[gp], jnp.concatenate(stacked, axis=1), preferred_element_type=F32))
        ya = []
        for c, rows in enumerate(chunk_rows):
            mixed = jnp.concatenate([col[:, c * LANES:(c + 1) * LANES] for col in cols], axis=1) + bmat
            ya.append((u_ref[rows, :].astype(F32) * mixed).astype(BF16))
        return jnp.concatenate(ya, axis=0)

    def merge(r, ya):
        rows = slice(r * MIX_SUB, (r + 1) * MIX_SUB)
        pa = jnp.dot(ya, wpa_ref[...], preferred_element_type=F32)
        pb = jnp.dot(yb_ref[rows, :], wpb_ref[...], preferred_element_type=F32)
        return (ga_ref[rows, :].astype(F32) * pa + gb_ref[rows, :].astype(F32) * pb).astype(BF16)

    def project(r, merged):
        rows = slice(r * MIX_SUB, (r + 1) * MIX_SUB)
        y = jnp.dot(merged, wo_ref[...], preferred_element_type=F32)
        o_ref[rows, :] = x_ref[rows, :] + _rms(y, gpost_ref[...])

    n_sub = tm // MIX_SUB
    ya, merged = {}, {}
    for r in range(n_sub + 2):
        if r < n_sub:
            ya[r] = gating(r)
        if 0 <= r - 1 < n_sub:
            merged[r - 1] = merge(r - 1, ya.pop(r - 1))
        if 0 <= r - 2 < n_sub:
            project(r - 2, merged.pop(r - 2))


def _mix_out(x, u, v, yb, ga, gb, ws, bmat, wpa, wpb, wo, gpost, tm):
    t = x.shape[0]
    row = lambda width: pl.BlockSpec((tm, width), lambda i: (i, 0))
    return pl.pallas_call(
        _mix_out_kernel,
        grid=(t // tm,),
        in_specs=[row(D_MODEL), row(GM_WIDTH), row(GM_WIDTH), row(ATTN_WIDTH), row(D_MODEL), row(D_MODEL),
                  _const_spec(ws.shape), _const_spec(bmat.shape), _const_spec(wpa.shape),
                  _const_spec(wpb.shape), _const_spec(wo.shape), _const_spec((1, D_MODEL))],
        out_specs=row(D_MODEL),
        out_shape=jax.ShapeDtypeStruct((t, D_MODEL), F32),
        compiler_params=pltpu.CompilerParams(dimension_semantics=("arbitrary",),
                                             vmem_limit_bytes=VMEM_LIMIT_BYTES),
        name="mix_out",
    )(x, u, v, yb, ga, gb, ws, bmat, wpa, wpb, wo, gpost)


def _ffn_kernel(x_ref, gpre_ref, wgu_ref, wd_ref, gpost_ref, o_ref):
    tm = x_ref.shape[0]
    items = [(r, c) for r in range(tm // FFN_SUB) for c in range(len(FFN_CHUNKS))]
    h, f = {}, {}

    def up(i):
        r, c = items[i]
        rows = slice(r * FFN_SUB, (r + 1) * FFN_SUB)
        lo, width = FFN_CHUNKS[c]
        if r not in h:
            h[r] = _rms(x_ref[rows, :], gpre_ref[...]).astype(BF16)
        gate = jnp.dot(h[r], wgu_ref[:, lo:lo + width], preferred_element_type=F32)
        lin = jnp.dot(h[r], wgu_ref[:, D_FF + lo:D_FF + lo + width], preferred_element_type=F32)
        return gate, lin

    def down(i, gate, lin):
        r, c = items[i]
        lo, width = FFN_CHUNKS[c]
        act = (jax.nn.silu(gate) * lin).astype(BF16)
        part = jnp.dot(act, wd_ref[lo:lo + width, :], preferred_element_type=F32)
        f[r] = part if c == 0 else f[r] + part

    def finish(r):
        rows = slice(r * FFN_SUB, (r + 1) * FFN_SUB)
        o_ref[rows, :] = x_ref[rows, :] + _rms(f.pop(r), gpost_ref[...])

    staged = {}
    for i in range(len(items) + 2):
        if i < len(items):
            staged[i] = up(i)
        if 0 <= i - 1 < len(items):
            down(i - 1, *staged.pop(i - 1))
        if 0 <= i - 2 < len(items) and items[i - 2][1] == len(FFN_CHUNKS) - 1:
            finish(items[i - 2][0])


def _ffn(x, gpre, wgu, wd, gpost, tm):
    t = x.shape[0]
    row = pl.BlockSpec((tm, D_MODEL), lambda i: (i, 0))
    return pl.pallas_call(
        _ffn_kernel,
        grid=(t // tm,),
        in_specs=[row, _const_spec((1, D_MODEL)), _const_spec(wgu.shape), _const_spec(wd.shape),
                  _const_spec((1, D_MODEL))],
        out_specs=row,
        out_shape=jax.ShapeDtypeStruct((t, D_MODEL), F32),
        compiler_params=pltpu.CompilerParams(dimension_semantics=("arbitrary",),
                                             vmem_limit_bytes=VMEM_LIMIT_BYTES),
        name="ffn",
    )(x, gpre, wgu, wd, gpost)


def _rope_tables(positions):
    half = HEAD_DIM // 2
    inv_freq = 1.0 / (ROPE_THETA ** (jnp.arange(0, HEAD_DIM, 2, dtype=F32) / HEAD_DIM))
    ang = positions.astype(F32).reshape(-1, 1) * inv_freq
    cos = jnp.cos(ang)
    sin = jnp.sin(ang)
    cos_t = jnp.tile(cos, (1, LANES // half))
    sin_t = jnp.tile(jnp.concatenate([-sin, sin], axis=1), (1, HEADS_PER_TILE))
    return cos_t, sin_t


def kernel(x, positions, w_in, w_s, b_s, ln_v_g, ln_v_b, w_proj_a, w_proj_b, w_out,
           g_mix_pre, g_mix_post, g_ffn_pre, g_ffn_post, w_gate_up, w_down):
    b, s, d = x.shape
    depth = w_in.shape[0]
    t = b * s
    cos_t, sin_t = _rope_tables(positions)
    xf = x.reshape(t, d)
    for l in range(depth):
        ws = w_s[l].reshape(GM_GROUPS // 2, 2, GM_CHUNK, GM_CHUNK).transpose(0, 2, 1, 3)
        ws = ws.reshape(GM_GROUPS // 2, GM_CHUNK, 2 * GM_CHUNK).astype(BF16)
        bmat = jnp.repeat(b_s[l].T, GM_GROUP_DIM, axis=1)
        u, v, q, k, va, ga, gb = _in_proj(
            xf, g_mix_pre[l].reshape(1, d), w_in[l].astype(BF16), cos_t, sin_t,
            ln_v_g[l].reshape(1, -1), ln_v_b[l].reshape(1, -1), tm=1024)
        yb = _moba(q.reshape(b, s, -1), k.reshape(b, s, -1), va.reshape(b, s, -1)).reshape(t, -1)
        xf = _mix_out(xf, u, v, yb, ga, gb, ws, bmat, w_proj_a[l].astype(BF16),
                      w_proj_b[l].astype(BF16), w_out[l].astype(BF16),
                      g_mix_post[l].reshape(1, d), tm=1024)
        xf = _ffn(xf, g_ffn_pre[l].reshape(1, d), w_gate_up[l].astype(BF16),
                  w_down[l].astype(BF16), g_ffn_post[l].reshape(1, d), tm=512)
    return xf.reshape(b, s, d)
```

```python
import jax
import jax.numpy as jnp
from jax import lax
from jax.experimental import pallas as pl
from jax.experimental.pallas import tpu as pltpu

D_MODEL = 1024
GM_GROUPS = 8
GM_GROUP_DIM = 64
GM_WIDTH = GM_GROUPS * GM_GROUP_DIM
GM_CHUNK = 128
ATTN_HEADS = 8
HEAD_DIM = 64
ATTN_WIDTH = ATTN_HEADS * HEAD_DIM
MOBA_BLOCK = 256
MOBA_TOPK = 3
ROPE_THETA = 10000.0
D_FF = 2816
IN_WIDTH = 2 * GM_WIDTH + 3 * ATTN_WIDTH + 2 * D_MODEL
NORM_EPS = 1e-6
NEG_INF = -1e30
LOG2_E = 1.4426950408889634

LANES = 128
HEADS_PER_TILE = LANES // HEAD_DIM
VMEM_LIMIT_BYTES = 56 * 1024 * 1024
MXU_TILE = 256

IN_PROJ_SUB = 256
FFN_SUB = 256
MIX_SUB = 256
FFN_CHUNKS = ((0, 5 * MXU_TILE), (5 * MXU_TILE, D_FF - 5 * MXU_TILE))

F32 = jnp.float32
BF16 = jnp.bfloat16


def _rms(x, g):
    return x * lax.rsqrt(jnp.mean(x * x, axis=-1, keepdims=True) + NORM_EPS) * g


def _layer_spec(stacked, l):
    shape = stacked.shape[1:]
    return pl.BlockSpec((None,) + shape, lambda *_: (l,) + (0,) * len(shape),
                        pipeline_mode=pl.Buffered(1))


def _in_proj_kernel(x_ref, g_ref, w_ref, cos_ref, sin_ref, lng_ref, lnb_ref,
                    u_ref, v_ref, q_ref, k_ref, va_ref, ga_ref, gb_ref):
    tm = x_ref.shape[0]
    first_half = (lax.broadcasted_iota(jnp.int32, (IN_PROJ_SUB, LANES), 1) % HEAD_DIM) < HEAD_DIM // 2

    def gelu_out(out_ref):
        def ep(z, rows):
            out_ref[rows, :] = jax.nn.gelu(z).astype(BF16)
        return ep

    def gelu_ln_out(z, rows):
        v = jax.nn.gelu(z)
        mu = jnp.mean(v, axis=-1, keepdims=True)
        vc = v - mu
        var = jnp.mean(vc * vc, axis=-1, keepdims=True)
        v_ref[rows, :] = (vc * lax.rsqrt(var + NORM_EPS) * lng_ref[...] + lnb_ref[...]).astype(BF16)

    def rope_out(out_ref, scale):
        def ep(z, rows):
            cos = cos_ref[rows, :]
            sin = sin_ref[rows, :]
            for c in range(ATTN_WIDTH // LANES):
                zc = z[:, c * LANES:(c + 1) * LANES]
                partner = jnp.where(first_half,
                                    pltpu.roll(zc, LANES - HEAD_DIM // 2, 1),
                                    pltpu.roll(zc, HEAD_DIM // 2, 1))
                r = zc * cos + partner * sin
                if scale != 1.0:
                    r = r * scale
                out_ref[rows, c * LANES:(c + 1) * LANES] = r.astype(BF16)
        return ep

    def cast_out(z, rows):
        va_ref[rows, :] = z.astype(BF16)

    def sigmoid_out(out_ref):
        def ep(z, rows):
            out_ref[rows, :] = jax.nn.sigmoid(z).astype(BF16)
        return ep

    o_q = 2 * GM_WIDTH
    o_g = o_q + 3 * ATTN_WIDTH
    sections = [
        (0, GM_WIDTH, gelu_out(u_ref)),
        (o_g, D_MODEL, sigmoid_out(ga_ref)),
        (GM_WIDTH, GM_WIDTH, gelu_ln_out),
        (o_g + D_MODEL, D_MODEL, sigmoid_out(gb_ref)),
        (o_q, ATTN_WIDTH, rope_out(q_ref, HEAD_DIM ** -0.5 * LOG2_E)),
        (o_q + ATTN_WIDTH, ATTN_WIDTH, rope_out(k_ref, 1.0)),
        (o_q + 2 * ATTN_WIDTH, ATTN_WIDTH, cast_out),
    ]
    items = [(r, sec) for r in range(tm // IN_PROJ_SUB) for sec in sections]

    h = {}
    pending = None
    for r, (lo, width, ep) in items:
        rows = slice(r * IN_PROJ_SUB, (r + 1) * IN_PROJ_SUB)
        if r not in h:
            h[r] = _rms(x_ref[rows, :], g_ref[...]).astype(BF16)
        z = jnp.dot(h[r], w_ref[:, lo:lo + width], preferred_element_type=F32)
        if pending is not None:
            pending[0](pending[1], pending[2])
        pending = (ep, z, rows)
    pending[0](pending[1], pending[2])


def _in_proj(x, g, w, cos, sin, lng, lnb, l, tm):
    t = x.shape[0]
    row = lambda width: pl.BlockSpec((tm, width), lambda i: (i, 0))
    widths = (GM_WIDTH, GM_WIDTH, ATTN_WIDTH, ATTN_WIDTH, ATTN_WIDTH, D_MODEL, D_MODEL)
    return pl.pallas_call(
        _in_proj_kernel,
        grid=(t // tm,),
        in_specs=[row(D_MODEL), _layer_spec(g, l), _layer_spec(w, l),
                  row(LANES), row(LANES), _layer_spec(lng, l), _layer_spec(lnb, l)],
        out_specs=[row(wd) for wd in widths],
        out_shape=[jax.ShapeDtypeStruct((t, wd), BF16) for wd in widths],
        compiler_params=pltpu.CompilerParams(dimension_semantics=("arbitrary",),
                                             vmem_limit_bytes=VMEM_LIMIT_BYTES),
        name="in_proj",
    )(x, g, w, cos, sin, lng, lnb)


def _moba_kernel(q_ref, k_ref, v_ref, o_ref, vt_ref, kbar_ref):
    nb = k_ref.shape[1] // MOBA_BLOCK
    lane = lax.broadcasted_iota(jnp.int32, (1, LANES), 1)
    key_id = lax.broadcasted_iota(jnp.int32, (MOBA_BLOCK, MOBA_BLOCK), 0)
    qry_id = lax.broadcasted_iota(jnp.int32, (MOBA_BLOCK, MOBA_BLOCK), 1)
    causal = key_id <= qry_id
    jidx = lax.broadcasted_iota(jnp.int32, (nb, MOBA_BLOCK), 0)
    drow = lax.broadcasted_iota(jnp.int32, (LANES, 1), 0)
    nt = (((1,), (1,)), ((), ()))

    vt = v_ref[0].astype(F32).T
    for hh in range(HEADS_PER_TILE):
        vt_ref[hh] = jnp.where((drow // HEAD_DIM) == hh, vt, 1.0).astype(BF16)
    for j in range(nb):
        kb = k_ref[0, j * MOBA_BLOCK:(j + 1) * MOBA_BLOCK, :]
        kbar = jnp.mean(kb.astype(F32), axis=0, keepdims=True)
        hi = kbar.astype(BF16)
        lo = (kbar - hi.astype(F32)).astype(BF16)
        kbar_ref[j:j + 1, :] = hi.astype(F32)
        kbar_ref[nb + j:nb + j + 1, :] = lo.astype(F32)
    kbar2 = kbar_ref[...].astype(BF16)

    def scores(n, hh):
        past = n * MOBA_BLOCK
        q = q_ref[0, past:past + MOBA_BLOCK, :]
        own = (lane // HEAD_DIM) == hh
        q_own = jnp.where(own, q, jnp.zeros_like(q))
        if n > MOBA_TOPK:
            g2 = lax.dot_general(kbar2, q_own, nt, preferred_element_type=F32)
            gate = jnp.where(jidx < n, g2[:nb] + g2[nb:], -jnp.inf)
            cnt = jnp.zeros((nb, MOBA_BLOCK), jnp.int32)
            for jp in range(n):
                gj = gate[jp:jp + 1, :]
                ahead = (gj > gate) | ((gj == gate) & (jp < jidx))
                cnt = cnt + jnp.where(ahead, 1, 0)
            bias = jnp.where(cnt < MOBA_TOPK, 0.0, NEG_INF)
        else:
            bias = None
        st = lax.dot_general(k_ref[0, :past + MOBA_BLOCK, :], q_own, nt,
                             preferred_element_type=F32)
        return st, bias

    def maxes(n, st, bias):
        past = n * MOBA_BLOCK
        s_blocks = [st[j * MOBA_BLOCK:(j + 1) * MOBA_BLOCK] for j in range(n)]
        s_blocks.append(jnp.where(causal, st[past:], NEG_INF))
        m = None
        for j, sj in enumerate(s_blocks):
            mj = jnp.max(sj, axis=0, keepdims=True)
            if bias is not None and j < n:
                mj = mj + bias[j:j + 1, :]
            m = mj if m is None else jnp.maximum(m, mj)
        return s_blocks, m, bias

    def probs(n, s_blocks, m, bias):
        p_blocks = []
        for j, sj in enumerate(s_blocks):
            shift = m if (bias is None or j == n) else m - bias[j:j + 1, :]
            p_blocks.append(jnp.exp2(sj - shift).astype(BF16))
        return p_blocks[0] if n == 0 else jnp.concatenate(p_blocks, axis=0)

    def output(n, hh, pt):
        acc = jnp.dot(vt_ref[hh, :, :(n + 1) * MOBA_BLOCK], pt, preferred_element_type=F32)
        ones_row = HEAD_DIM * (1 - hh)
        return acc / acc[ones_row:ones_row + 1, :]

    items = [(n, hh) for n in range(nb) for hh in range(HEADS_PER_TILE)]
    stage_a, stage_b, outs = {}, {}, {}
    for i in range(len(items) + 2):
        if i < len(items):
            stage_a[i] = maxes(items[i][0], *scores(*items[i]))
        if 0 <= i - 1 < len(items):
            stage_b[i - 1] = probs(items[i - 1][0], *stage_a.pop(i - 1))
        if 0 <= i - 2 < len(items):
            n, hh = items[i - 2]
            outs[hh] = output(n, hh, stage_b.pop(i - 2))
            if hh == HEADS_PER_TILE - 1:
                out_t = jnp.where(drow < HEAD_DIM, outs[0], outs[1])
                o_ref[0, n * MOBA_BLOCK:(n + 1) * MOBA_BLOCK, :] = out_t.T.astype(BF16)


def _moba(q, k, v):
    b, s, w = q.shape
    tiles = w // LANES
    nb = s // MOBA_BLOCK
    spec = pl.BlockSpec((1, s, LANES), lambda bp: (bp // tiles, 0, bp % tiles))
    return pl.pallas_call(
        _moba_kernel,
        grid=(b * tiles,),
        in_specs=[spec, spec, spec],
        out_specs=spec,
        out_shape=jax.ShapeDtypeStruct((b, s, w), BF16),
        scratch_shapes=[pltpu.VMEM((HEADS_PER_TILE, LANES, s), BF16),
                        pltpu.VMEM((2 * nb, LANES), F32)],
        compiler_params=pltpu.CompilerParams(dimension_semantics=("arbitrary",),
                                             vmem_limit_bytes=VMEM_LIMIT_BYTES),
        name="moba",
    )(q, k, v)


def _mix_out_kernel(x_ref, u_ref, v_ref, yb_ref, ga_ref, gb_ref, ws_ref, bmat_ref,
                    wpa_ref, wpb_ref, wo_ref, gpost_ref, o_ref):
    tm = x_ref.shape[0]
    lane = lax.broadcasted_iota(jnp.int32, (1, LANES), 1)
    first = lane < GM_GROUP_DIM
    zero = jnp.zeros((GM_CHUNK, LANES), BF16)
    wi = lax.broadcasted_iota(jnp.int32, (GM_CHUNK, 2 * GM_CHUNK), 0)
    wj = lax.broadcasted_iota(jnp.int32, (GM_CHUNK, 2 * GM_CHUNK), 1) % GM_CHUNK
    tril = wj <= wi
    bmat = bmat_ref[...]
    w_tril = [jnp.where(tril, ws_ref[gp], jnp.zeros_like(ws_ref[gp])) for gp in range(GM_WIDTH // LANES)]

    def gating(r):
        n_chunks = MIX_SUB // GM_CHUNK
        chunk_rows = [slice(r * MIX_SUB + c * GM_CHUNK, r * MIX_SUB + (c + 1) * GM_CHUNK)
                      for c in range(n_chunks)]
        cols = []
        for gp in range(GM_WIDTH // LANES):
            stacked = []
            for rows in chunk_rows:
                vp = v_ref[rows, gp * LANES:(gp + 1) * LANES]
                stacked.append(jnp.concatenate([jnp.where(first, vp, zero), jnp.where(first, zero, vp)],
                                               axis=0))
            cols.append(jnp.dot(w_tril[gp], jnp.concatenate(stacked, axis=1), preferred_element_type=F32))
        ya = []
        for c, rows in enumerate(chunk_rows):
            mixed = jnp.concatenate([col[:, c * LANES:(c + 1) * LANES] for col in cols], axis=1) + bmat
            ya.append((u_ref[rows, :].astype(F32) * mixed).astype(BF16))
        return jnp.concatenate(ya, axis=0)

    def merge(r, ya):
        rows = slice(r * MIX_SUB, (r + 1) * MIX_SUB)
        pa = jnp.dot(ya, wpa_ref[...], preferred_element_type=F32)
        pb = jnp.dot(yb_ref[rows, :], wpb_ref[...], preferred_element_type=F32)
        return (ga_ref[rows, :].astype(F32) * pa + gb_ref[rows, :].astype(F32) * pb).astype(BF16)

    def project(r, merged):
        rows = slice(r * MIX_SUB, (r + 1) * MIX_SUB)
        y = jnp.dot(merged, wo_ref[...], preferred_element_type=F32)
        o_ref[rows, :] = x_ref[rows, :] + _rms(y, gpost_ref[...])

    n_sub = tm // MIX_SUB
    ya, merged = {}, {}
    for r in range(n_sub + 2):
        if r < n_sub:
            ya[r] = gating(r)
        if 0 <= r - 1 < n_sub:
            merged[r - 1] = merge(r - 1, ya.pop(r - 1))
        if 0 <= r - 2 < n_sub:
            project(r - 2, merged.pop(r - 2))


def _mix_out(x, u, v, yb, ga, gb, ws, bmat, wpa, wpb, wo, gpost, l, tm):
    t = x.shape[0]
    row = lambda width: pl.BlockSpec((tm, width), lambda i: (i, 0))
    return pl.pallas_call(
        _mix_out_kernel,
        grid=(t // tm,),
        in_specs=[row(D_MODEL), row(GM_WIDTH), row(GM_WIDTH), row(ATTN_WIDTH), row(D_MODEL), row(D_MODEL),
                  _layer_spec(ws, l), _layer_spec(bmat, l), _layer_spec(wpa, l),
                  _layer_spec(wpb, l), _layer_spec(wo, l), _layer_spec(gpost, l)],
        out_specs=row(D_MODEL),
        out_shape=jax.ShapeDtypeStruct((t, D_MODEL), F32),
        compiler_params=pltpu.CompilerParams(dimension_semantics=("arbitrary",),
                                             vmem_limit_bytes=VMEM_LIMIT_BYTES),
        name="mix_out",
    )(x, u, v, yb, ga, gb, ws, bmat, wpa, wpb, wo, gpost)


def _ffn_kernel(x_ref, gpre_ref, wgu_ref, wd_ref, gpost_ref, o_ref):
    tm = x_ref.shape[0]
    items = [(r, c) for r in range(tm // FFN_SUB) for c in range(len(FFN_CHUNKS))]
    h, f = {}, {}

    def up(i):
        r, c = items[i]
        rows = slice(r * FFN_SUB, (r + 1) * FFN_SUB)
        lo, width = FFN_CHUNKS[c]
        if r not in h:
            h[r] = _rms(x_ref[rows, :], gpre_ref[...]).astype(BF16)
        gate = jnp.dot(h[r], wgu_ref[:, lo:lo + width], preferred_element_type=F32)
        lin = jnp.dot(h[r], wgu_ref[:, D_FF + lo:D_FF + lo + width], preferred_element_type=F32)
        return gate, lin

    def down(i, gate, lin):
        r, c = items[i]
        lo, width = FFN_CHUNKS[c]
        act = (jax.nn.silu(gate) * lin).astype(BF16)
        part = jnp.dot(act, wd_ref[lo:lo + width, :], preferred_element_type=F32)
        f[r] = part if c == 0 else f[r] + part

    def finish(r):
        rows = slice(r * FFN_SUB, (r + 1) * FFN_SUB)
        o_ref[rows, :] = x_ref[rows, :] + _rms(f.pop(r), gpost_ref[...])

    staged = {}
    for i in range(len(items) + 2):
        if i < len(items):
            staged[i] = up(i)
        if 0 <= i - 1 < len(items):
            down(i - 1, *staged.pop(i - 1))
        if 0 <= i - 2 < len(items) and items[i - 2][1] == len(FFN_CHUNKS) - 1:
            finish(items[i - 2][0])


def _ffn(x, gpre, wgu, wd, gpost, l, tm):
    t = x.shape[0]
    row = pl.BlockSpec((tm, D_MODEL), lambda i: (i, 0))
    return pl.pallas_call(
        _ffn_kernel,
        grid=(t // tm,),
        in_specs=[row, _layer_spec(gpre, l), _layer_spec(wgu, l), _layer_spec(wd, l),
                  _layer_spec(gpost, l)],
        out_specs=row,
        out_shape=jax.ShapeDtypeStruct((t, D_MODEL), F32),
        compiler_params=pltpu.CompilerParams(dimension_semantics=("arbitrary",),
                                             vmem_limit_bytes=VMEM_LIMIT_BYTES),
        name="ffn",
    )(x, gpre, wgu, wd, gpost)


def _rope_tables(positions):
    half = HEAD_DIM // 2
    inv_freq = 1.0 / (ROPE_THETA ** (jnp.arange(0, HEAD_DIM, 2, dtype=F32) / HEAD_DIM))
    ang = positions.astype(F32).reshape(-1, 1) * inv_freq
    cos = jnp.cos(ang)
    sin = jnp.sin(ang)
    cos_t = jnp.tile(cos, (1, LANES // half))
    sin_t = jnp.tile(jnp.concatenate([-sin, sin], axis=1), (1, HEADS_PER_TILE))
    return cos_t, sin_t


def kernel(x, positions, w_in, w_s, b_s, ln_v_g, ln_v_b, w_proj_a, w_proj_b, w_out,
           g_mix_pre, g_mix_post, g_ffn_pre, g_ffn_post, w_gate_up, w_down):
    b, s, d = x.shape
    depth = w_in.shape[0]
    t = b * s
    cos_t, sin_t = _rope_tables(positions)
    xf = x.reshape(t, d)
    gains = lambda g: g.reshape(depth, 1, -1)
    w_in_b, wpa_b, wpb_b, wo_b = (w.astype(BF16) for w in (w_in, w_proj_a, w_proj_b, w_out))
    wgu_b, wd_b = w_gate_up.astype(BF16), w_down.astype(BF16)
    ws = w_s.reshape(depth, GM_GROUPS // 2, 2, GM_CHUNK, GM_CHUNK).transpose(0, 1, 3, 2, 4)
    ws = ws.reshape(depth, GM_GROUPS // 2, GM_CHUNK, 2 * GM_CHUNK).astype(BF16)
    bmat = jnp.repeat(b_s.transpose(0, 2, 1), GM_GROUP_DIM, axis=2)
    for l in range(depth):
        u, v, q, k, va, ga, gb = _in_proj(xf, gains(g_mix_pre), w_in_b, cos_t, sin_t,
                                          gains(ln_v_g), gains(ln_v_b), l, tm=1024)
        yb = _moba(q.reshape(b, s, -1), k.reshape(b, s, -1), va.reshape(b, s, -1)).reshape(t, -1)
        xf = _mix_out(xf, u, v, yb, ga, gb, ws, bmat, wpa_b, wpb_b, wo_b, gains(g_mix_post), l, tm=1024)
        xf = _ffn(xf, gains(g_ffn_pre), wgu_b, wd_b, gains(g_ffn_post), l, tm=512)
    return xf.reshape(b, s, d)
```

```python
import jax
import jax.numpy as jnp
from jax import lax
from jax.experimental import pallas as pl
from jax.experimental.pallas import tpu as pltpu

D_MODEL = 1024
GM_GROUPS = 8
GM_GROUP_DIM = 64
GM_WIDTH = GM_GROUPS * GM_GROUP_DIM
GM_CHUNK = 128
ATTN_HEADS = 8
HEAD_DIM = 64
ATTN_WIDTH = ATTN_HEADS * HEAD_DIM
MOBA_BLOCK = 256
MOBA_TOPK = 3
ROPE_THETA = 10000.0
D_FF = 2816
IN_WIDTH = 2 * GM_WIDTH + 3 * ATTN_WIDTH + 2 * D_MODEL
NORM_EPS = 1e-6
NEG_INF = -1e30
LOG2_E = 1.4426950408889634

LANES = 128
HEADS_PER_TILE = LANES // HEAD_DIM
VMEM_LIMIT_BYTES = 56 * 1024 * 1024
MXU_TILE = 256

IN_PROJ_SUB = 256
TOKEN_SUB = 256
FFN_CHUNKS = ((0, 5 * MXU_TILE), (5 * MXU_TILE, D_FF - 5 * MXU_TILE))

F32 = jnp.float32
BF16 = jnp.bfloat16


def _rms(x, g):
    return x * lax.rsqrt(jnp.mean(x * x, axis=-1, keepdims=True) + NORM_EPS) * g


def _layer_spec(stacked, l):
    shape = stacked.shape[1:]
    return pl.BlockSpec((None,) + shape, lambda *_: (l,) + (0,) * len(shape),
                        pipeline_mode=pl.Buffered(1))


def _in_proj_kernel(x_ref, g_ref, w_ref, cos_ref, sin_ref, lng_ref, lnb_ref,
                    u_ref, v_ref, q_ref, k_ref, va_ref, ga_ref, gb_ref):
    tm = x_ref.shape[0]
    first_half = (lax.broadcasted_iota(jnp.int32, (IN_PROJ_SUB, LANES), 1) % HEAD_DIM) < HEAD_DIM // 2

    def gelu_out(out_ref):
        def ep(z, rows):
            out_ref[rows, :] = jax.nn.gelu(z).astype(BF16)
        return ep

    def gelu_ln_out(z, rows):
        v = jax.nn.gelu(z)
        mu = jnp.mean(v, axis=-1, keepdims=True)
        vc = v - mu
        var = jnp.mean(vc * vc, axis=-1, keepdims=True)
        v_ref[rows, :] = (vc * lax.rsqrt(var + NORM_EPS) * lng_ref[...] + lnb_ref[...]).astype(BF16)

    def rope_out(out_ref, scale):
        def ep(z, rows):
            cos = cos_ref[rows, :]
            sin = sin_ref[rows, :]
            for c in range(ATTN_WIDTH // LANES):
                zc = z[:, c * LANES:(c + 1) * LANES]
                partner = jnp.where(first_half,
                                    pltpu.roll(zc, LANES - HEAD_DIM // 2, 1),
                                    pltpu.roll(zc, HEAD_DIM // 2, 1))
                r = zc * cos + partner * sin
                if scale != 1.0:
                    r = r * scale
                out_ref[rows, c * LANES:(c + 1) * LANES] = r.astype(BF16)
        return ep

    def cast_out(z, rows):
        va_ref[rows, :] = z.astype(BF16)

    def sigmoid_out(out_ref):
        def ep(z, rows):
            out_ref[rows, :] = jax.nn.sigmoid(z).astype(BF16)
        return ep

    o_q = 2 * GM_WIDTH
    o_g = o_q + 3 * ATTN_WIDTH
    sections = [
        (0, GM_WIDTH, gelu_out(u_ref)),
        (o_g, D_MODEL, sigmoid_out(ga_ref)),
        (GM_WIDTH, GM_WIDTH, gelu_ln_out),
        (o_g + D_MODEL, D_MODEL, sigmoid_out(gb_ref)),
        (o_q, ATTN_WIDTH, rope_out(q_ref, HEAD_DIM ** -0.5 * LOG2_E)),
        (o_q + ATTN_WIDTH, ATTN_WIDTH, rope_out(k_ref, 1.0)),
        (o_q + 2 * ATTN_WIDTH, ATTN_WIDTH, cast_out),
    ]
    items = [(r, sec) for r in range(tm // IN_PROJ_SUB) for sec in sections]

    h = {}
    pending = None
    for r, (lo, width, ep) in items:
        rows = slice(r * IN_PROJ_SUB, (r + 1) * IN_PROJ_SUB)
        if r not in h:
            h[r] = _rms(x_ref[rows, :], g_ref[...]).astype(BF16)
        z = jnp.dot(h[r], w_ref[:, lo:lo + width], preferred_element_type=F32)
        if pending is not None:
            pending[0](pending[1], pending[2])
        pending = (ep, z, rows)
    pending[0](pending[1], pending[2])


def _in_proj(x, g, w, cos, sin, lng, lnb, l, tm):
    t = x.shape[0]
    row = lambda width: pl.BlockSpec((tm, width), lambda i: (i, 0))
    widths = (GM_WIDTH, GM_WIDTH, ATTN_WIDTH, ATTN_WIDTH, ATTN_WIDTH, D_MODEL, D_MODEL)
    return pl.pallas_call(
        _in_proj_kernel,
        grid=(t // tm,),
        in_specs=[row(D_MODEL), _layer_spec(g, l), _layer_spec(w, l),
                  row(LANES), row(LANES), _layer_spec(lng, l), _layer_spec(lnb, l)],
        out_specs=[row(wd) for wd in widths],
        out_shape=[jax.ShapeDtypeStruct((t, wd), BF16) for wd in widths],
        compiler_params=pltpu.CompilerParams(dimension_semantics=("arbitrary",),
                                             vmem_limit_bytes=VMEM_LIMIT_BYTES),
        name="in_proj",
    )(x, g, w, cos, sin, lng, lnb)


def _moba_kernel(q_ref, k_ref, v_ref, o_ref, vt_ref, kbar_ref):
    nb = k_ref.shape[1] // MOBA_BLOCK
    lane = lax.broadcasted_iota(jnp.int32, (1, LANES), 1)
    key_id = lax.broadcasted_iota(jnp.int32, (MOBA_BLOCK, MOBA_BLOCK), 0)
    qry_id = lax.broadcasted_iota(jnp.int32, (MOBA_BLOCK, MOBA_BLOCK), 1)
    causal = key_id <= qry_id
    jidx = lax.broadcasted_iota(jnp.int32, (nb, MOBA_BLOCK), 0)
    drow = lax.broadcasted_iota(jnp.int32, (LANES, 1), 0)
    nt = (((1,), (1,)), ((), ()))

    vt = v_ref[0].astype(F32).T
    for hh in range(HEADS_PER_TILE):
        vt_ref[hh] = jnp.where((drow // HEAD_DIM) == hh, vt, 1.0).astype(BF16)
    for j in range(nb):
        kb = k_ref[0, j * MOBA_BLOCK:(j + 1) * MOBA_BLOCK, :]
        kbar = jnp.mean(kb.astype(F32), axis=0, keepdims=True)
        hi = kbar.astype(BF16)
        lo = (kbar - hi.astype(F32)).astype(BF16)
        kbar_ref[j:j + 1, :] = hi.astype(F32)
        kbar_ref[nb + j:nb + j + 1, :] = lo.astype(F32)
    kbar2 = kbar_ref[...].astype(BF16)

    def scores(n, hh):
        past = n * MOBA_BLOCK
        q = q_ref[0, past:past + MOBA_BLOCK, :]
        own = (lane // HEAD_DIM) == hh
        q_own = jnp.where(own, q, jnp.zeros_like(q))
        if n > MOBA_TOPK:
            g2 = lax.dot_general(kbar2, q_own, nt, preferred_element_type=F32)
            gate = jnp.where(jidx < n, g2[:nb] + g2[nb:], -jnp.inf)
            cnt = jnp.zeros((nb, MOBA_BLOCK), jnp.int32)
            for jp in range(n):
                gj = gate[jp:jp + 1, :]
                ahead = (gj > gate) | ((gj == gate) & (jp < jidx))
                cnt = cnt + jnp.where(ahead, 1, 0)
            bias = jnp.where(cnt < MOBA_TOPK, 0.0, NEG_INF)
        else:
            bias = None
        st = lax.dot_general(k_ref[0, :past + MOBA_BLOCK, :], q_own, nt,
                             preferred_element_type=F32)
        return st, bias

    def maxes(n, st, bias):
        past = n * MOBA_BLOCK
        s_blocks = [st[j * MOBA_BLOCK:(j + 1) * MOBA_BLOCK] for j in range(n)]
        s_blocks.append(jnp.where(causal, st[past:], NEG_INF))
        m = None
        for j, sj in enumerate(s_blocks):
            mj = jnp.max(sj, axis=0, keepdims=True)
            if bias is not None and j < n:
                mj = mj + bias[j:j + 1, :]
            m = mj if m is None else jnp.maximum(m, mj)
        return s_blocks, m, bias

    def probs(n, s_blocks, m, bias):
        p_blocks = []
        for j, sj in enumerate(s_blocks):
            shift = m if (bias is None or j == n) else m - bias[j:j + 1, :]
            p_blocks.append(jnp.exp2(sj - shift).astype(BF16))
        return p_blocks[0] if n == 0 else jnp.concatenate(p_blocks, axis=0)

    def output(n, hh, pt):
        acc = jnp.dot(vt_ref[hh, :, :(n + 1) * MOBA_BLOCK], pt, preferred_element_type=F32)
        ones_row = HEAD_DIM * (1 - hh)
        return acc / acc[ones_row:ones_row + 1, :]

    items = [(n, hh) for n in range(nb) for hh in range(HEADS_PER_TILE)]
    stage_a, stage_b, outs = {}, {}, {}
    for i in range(len(items) + 2):
        if i < len(items):
            stage_a[i] = maxes(items[i][0], *scores(*items[i]))
        if 0 <= i - 1 < len(items):
            stage_b[i - 1] = probs(items[i - 1][0], *stage_a.pop(i - 1))
        if 0 <= i - 2 < len(items):
            n, hh = items[i - 2]
            outs[hh] = output(n, hh, stage_b.pop(i - 2))
            if hh == HEADS_PER_TILE - 1:
                out_t = jnp.where(drow < HEAD_DIM, outs[0], outs[1])
                o_ref[0, n * MOBA_BLOCK:(n + 1) * MOBA_BLOCK, :] = out_t.T.astype(BF16)


def _moba(q, k, v):
    b, s, w = q.shape
    tiles = w // LANES
    nb = s // MOBA_BLOCK
    spec = pl.BlockSpec((1, s, LANES), lambda bp: (bp // tiles, 0, bp % tiles))
    return pl.pallas_call(
        _moba_kernel,
        grid=(b * tiles,),
        in_specs=[spec, spec, spec],
        out_specs=spec,
        out_shape=jax.ShapeDtypeStruct((b, s, w), BF16),
        scratch_shapes=[pltpu.VMEM((HEADS_PER_TILE, LANES, s), BF16),
                        pltpu.VMEM((2 * nb, LANES), F32)],
        compiler_params=pltpu.CompilerParams(dimension_semantics=("arbitrary",),
                                             vmem_limit_bytes=VMEM_LIMIT_BYTES),
        name="moba",
    )(q, k, v)


def _mix_ffn_kernel(x_ref, u_ref, v_ref, yb_ref, ga_ref, gb_ref, ws_ref, bmat_ref,
                    wpa_ref, wpb_ref, wo_ref, gmix_ref, gpre_ref, wgu_ref, wd_ref, gffn_ref, o_ref):
    tm = x_ref.shape[0]
    lane = lax.broadcasted_iota(jnp.int32, (1, LANES), 1)
    first = lane < GM_GROUP_DIM
    zero = jnp.zeros((GM_CHUNK, LANES), BF16)
    wi = lax.broadcasted_iota(jnp.int32, (GM_CHUNK, 2 * GM_CHUNK), 0)
    wj = lax.broadcasted_iota(jnp.int32, (GM_CHUNK, 2 * GM_CHUNK), 1) % GM_CHUNK
    tril = wj <= wi
    bmat = bmat_ref[...]
    w_tril = [jnp.where(tril, ws_ref[gp], jnp.zeros_like(ws_ref[gp])) for gp in range(GM_WIDTH // LANES)]

    def rows_of(r):
        return slice(r * TOKEN_SUB, (r + 1) * TOKEN_SUB)

    def gating(r, st):
        n_chunks = TOKEN_SUB // GM_CHUNK
        chunk_rows = [slice(r * TOKEN_SUB + c * GM_CHUNK, r * TOKEN_SUB + (c + 1) * GM_CHUNK)
                      for c in range(n_chunks)]
        cols = []
        for gp in range(GM_WIDTH // LANES):
            stacked = []
            for rows in chunk_rows:
                vp = v_ref[rows, gp * LANES:(gp + 1) * LANES]
                stacked.append(jnp.concatenate([jnp.where(first, vp, zero), jnp.where(first, zero, vp)],
                                               axis=0))
            cols.append(jnp.dot(w_tril[gp], jnp.concatenate(stacked, axis=1), preferred_element_type=F32))
        ya = []
        for c, rows in enumerate(chunk_rows):
            mixed = jnp.concatenate([col[:, c * LANES:(c + 1) * LANES] for col in cols], axis=1) + bmat
            ya.append((u_ref[rows, :].astype(F32) * mixed).astype(BF16))
        st["ya"] = jnp.concatenate(ya, axis=0)

    def merge(r, st):
        rows = rows_of(r)
        pa = jnp.dot(st.pop("ya"), wpa_ref[...], preferred_element_type=F32)
        pb = jnp.dot(yb_ref[rows, :], wpb_ref[...], preferred_element_type=F32)
        st["merged"] = (ga_ref[rows, :].astype(F32) * pa + gb_ref[rows, :].astype(F32) * pb).astype(BF16)

    def project(r, st):
        y = jnp.dot(st.pop("merged"), wo_ref[...], preferred_element_type=F32)
        x1 = x_ref[rows_of(r), :] + _rms(y, gmix_ref[...])
        st["x1"] = x1
        st["h"] = _rms(x1, gpre_ref[...]).astype(BF16)

    def up(c):
        lo, width = FFN_CHUNKS[c]

        def stage(r, st):
            st["gate"] = jnp.dot(st["h"], wgu_ref[:, lo:lo + width], preferred_element_type=F32)
            st["lin"] = jnp.dot(st["h"], wgu_ref[:, D_FF + lo:D_FF + lo + width], preferred_element_type=F32)
        return stage

    def down(c):
        lo, width = FFN_CHUNKS[c]

        def stage(r, st):
            act = (jax.nn.silu(st.pop("gate")) * st.pop("lin")).astype(BF16)
            part = jnp.dot(act, wd_ref[lo:lo + width, :], preferred_element_type=F32)
            st["f"] = part if c == 0 else st["f"] + part
        return stage

    def finish(r, st):
        o_ref[rows_of(r), :] = st.pop("x1") + _rms(st.pop("f"), gffn_ref[...])

    stages = [gating, merge, project]
    for c in range(len(FFN_CHUNKS)):
        stages += [up(c), down(c)]
    stages.append(finish)

    state = [{} for _ in range(tm // TOKEN_SUB)]
    for stage in stages:
        for r, st in enumerate(state):
            stage(r, st)


def _mix_ffn(x, u, v, yb, ga, gb, ws, bmat, wpa, wpb, wo, gmix, gpre, wgu, wd, gffn, l, tm):
    t = x.shape[0]
    row = lambda width: pl.BlockSpec((tm, width), lambda i: (i, 0))
    params = (ws, bmat, wpa, wpb, wo, gmix, gpre, wgu, wd, gffn)
    return pl.pallas_call(
        _mix_ffn_kernel,
        grid=(t // tm,),
        in_specs=[row(D_MODEL), row(GM_WIDTH), row(GM_WIDTH), row(ATTN_WIDTH), row(D_MODEL), row(D_MODEL)]
                 + [_layer_spec(p, l) for p in params],
        out_specs=row(D_MODEL),
        out_shape=jax.ShapeDtypeStruct((t, D_MODEL), F32),
        compiler_params=pltpu.CompilerParams(dimension_semantics=("arbitrary",),
                                             vmem_limit_bytes=VMEM_LIMIT_BYTES),
        name="mix_ffn",
    )(x, u, v, yb, ga, gb, *params)


def _rope_tables(positions):
    half = HEAD_DIM // 2
    inv_freq = 1.0 / (ROPE_THETA ** (jnp.arange(0, HEAD_DIM, 2, dtype=F32) / HEAD_DIM))
    ang = positions.astype(F32).reshape(-1, 1) * inv_freq
    cos = jnp.cos(ang)
    sin = jnp.sin(ang)
    cos_t = jnp.tile(cos, (1, LANES // half))
    sin_t = jnp.tile(jnp.concatenate([-sin, sin], axis=1), (1, HEADS_PER_TILE))
    return cos_t, sin_t


def kernel(x, positions, w_in, w_s, b_s, ln_v_g, ln_v_b, w_proj_a, w_proj_b, w_out,
           g_mix_pre, g_mix_post, g_ffn_pre, g_ffn_post, w_gate_up, w_down):
    b, s, d = x.shape
    depth = w_in.shape[0]
    t = b * s
    cos_t, sin_t = _rope_tables(positions)
    xf = x.reshape(t, d)
    gains = lambda g: g.reshape(depth, 1, -1)
    w_in_b, wpa_b, wpb_b, wo_b = (w.astype(BF16) for w in (w_in, w_proj_a, w_proj_b, w_out))
    wgu_b, wd_b = w_gate_up.astype(BF16), w_down.astype(BF16)
    ws = w_s.reshape(depth, GM_GROUPS // 2, 2, GM_CHUNK, GM_CHUNK).transpose(0, 1, 3, 2, 4)
    ws = ws.reshape(depth, GM_GROUPS // 2, GM_CHUNK, 2 * GM_CHUNK).astype(BF16)
    bmat = jnp.repeat(b_s.transpose(0, 2, 1), GM_GROUP_DIM, axis=2)
    for l in range(depth):
        u, v, q, k, va, ga, gb = _in_proj(xf, gains(g_mix_pre), w_in_b, cos_t, sin_t,
                                          gains(ln_v_g), gains(ln_v_b), l, tm=1024)
        yb = _moba(q.reshape(b, s, -1), k.reshape(b, s, -1), va.reshape(b, s, -1)).reshape(t, -1)
        xf = _mix_ffn(xf, u, v, yb, ga, gb, ws, bmat, wpa_b, wpb_b, wo_b, gains(g_mix_post),
                      gains(g_ffn_pre), wgu_b, wd_b, gains(g_ffn_post), l, tm=512)
    return xf.reshape(b, s, d)
```

```python
import jax
import jax.numpy as jnp
from jax import lax
from jax.experimental import pallas as pl
from jax.experimental.pallas import tpu as pltpu

D_MODEL = 1024
GM_GROUPS = 8
GM_GROUP_DIM = 64
GM_WIDTH = GM_GROUPS * GM_GROUP_DIM
GM_CHUNK = 128
ATTN_HEADS = 8
HEAD_DIM = 64
ATTN_WIDTH = ATTN_HEADS * HEAD_DIM
MOBA_BLOCK = 256
MOBA_TOPK = 3
ROPE_THETA = 10000.0
D_FF = 2816
IN_WIDTH = 2 * GM_WIDTH + 3 * ATTN_WIDTH + 2 * D_MODEL
NORM_EPS = 1e-6
NEG_INF = -1e30
LOG2_E = 1.4426950408889634

LANES = 128
HEADS_PER_TILE = LANES // HEAD_DIM
VMEM_LIMIT_BYTES = 56 * 1024 * 1024
MXU_TILE = 256

IN_PROJ_SUB = 256
TOKEN_SUB = 256
FFN_CHUNKS = ((0, 5 * MXU_TILE), (5 * MXU_TILE, D_FF - 5 * MXU_TILE))

F32 = jnp.float32
BF16 = jnp.bfloat16


def _rms(x, g):
    return x * lax.rsqrt(jnp.mean(x * x, axis=-1, keepdims=True) + NORM_EPS) * g


def _layer_spec(param, l):
    if l is None:
        return pl.BlockSpec(param.shape, lambda *_: (0,) * param.ndim, pipeline_mode=pl.Buffered(1))
    shape = param.shape[1:]
    return pl.BlockSpec((None,) + shape, lambda *_: (l,) + (0,) * len(shape),
                        pipeline_mode=pl.Buffered(1))


def _cast_plan(stacks, l, steps):
    in_specs, out_specs, out_shapes = [], [], []
    for w in stacks:
        _, rows, cols = w.shape
        rb = rows // steps
        in_specs.append(pl.BlockSpec((None, rb, cols), lambda i: (l, i, 0)))
        out_specs.append(pl.BlockSpec((rb, cols), lambda i: (i, 0)))
        out_shapes.append(jax.ShapeDtypeStruct((rows, cols), BF16))
    return in_specs, out_specs, out_shapes


def _with_casts(body, n_in, n_out, n_cast):
    def kernel_fn(*refs):
        ins, rest = refs[:n_in], refs[n_in:]
        srcs, rest = rest[:n_cast], rest[n_cast:]
        outs, rest = rest[:n_out], rest[n_out:]
        dsts, scratch = rest[:n_cast], rest[n_cast:]
        for src, dst in zip(srcs, dsts):
            dst[...] = src[...].astype(BF16)
        body(*ins, *outs, *scratch)
    return kernel_fn


def _in_proj_kernel(x_ref, g_ref, w_ref, cos_ref, sin_ref, lng_ref, lnb_ref,
                    u_ref, v_ref, q_ref, k_ref, va_ref, ga_ref, gb_ref):
    tm = x_ref.shape[0]
    first_half = (lax.broadcasted_iota(jnp.int32, (IN_PROJ_SUB, LANES), 1) % HEAD_DIM) < HEAD_DIM // 2

    def gelu_out(out_ref):
        def ep(z, rows):
            out_ref[rows, :] = jax.nn.gelu(z).astype(BF16)
        return ep

    def gelu_ln_out(z, rows):
        v = jax.nn.gelu(z)
        mu = jnp.mean(v, axis=-1, keepdims=True)
        vc = v - mu
        var = jnp.mean(vc * vc, axis=-1, keepdims=True)
        v_ref[rows, :] = (vc * lax.rsqrt(var + NORM_EPS) * lng_ref[...] + lnb_ref[...]).astype(BF16)

    def rope_out(out_ref, scale):
        def ep(z, rows):
            cos = cos_ref[rows, :]
            sin = sin_ref[rows, :]
            for c in range(ATTN_WIDTH // LANES):
                zc = z[:, c * LANES:(c + 1) * LANES]
                partner = jnp.where(first_half,
                                    pltpu.roll(zc, LANES - HEAD_DIM // 2, 1),
                                    pltpu.roll(zc, HEAD_DIM // 2, 1))
                r = zc * cos + partner * sin
                if scale != 1.0:
                    r = r * scale
                out_ref[rows, c * LANES:(c + 1) * LANES] = r.astype(BF16)
        return ep

    def cast_out(z, rows):
        va_ref[rows, :] = z.astype(BF16)

    def sigmoid_out(out_ref):
        def ep(z, rows):
            out_ref[rows, :] = jax.nn.sigmoid(z).astype(BF16)
        return ep

    o_q = 2 * GM_WIDTH
    o_g = o_q + 3 * ATTN_WIDTH
    sections = [
        (0, GM_WIDTH, gelu_out(u_ref)),
        (o_g, D_MODEL, sigmoid_out(ga_ref)),
        (GM_WIDTH, GM_WIDTH, gelu_ln_out),
        (o_g + D_MODEL, D_MODEL, sigmoid_out(gb_ref)),
        (o_q, ATTN_WIDTH, rope_out(q_ref, HEAD_DIM ** -0.5 * LOG2_E)),
        (o_q + ATTN_WIDTH, ATTN_WIDTH, rope_out(k_ref, 1.0)),
        (o_q + 2 * ATTN_WIDTH, ATTN_WIDTH, cast_out),
    ]
    items = [(r, sec) for r in range(tm // IN_PROJ_SUB) for sec in sections]

    h = {}
    pending = None
    for r, (lo, width, ep) in items:
        rows = slice(r * IN_PROJ_SUB, (r + 1) * IN_PROJ_SUB)
        if r not in h:
            h[r] = _rms(x_ref[rows, :], g_ref[...]).astype(BF16)
        z = jnp.dot(h[r], w_ref[:, lo:lo + width], preferred_element_type=F32)
        if pending is not None:
            pending[0](pending[1], pending[2])
        pending = (ep, z, rows)
    pending[0](pending[1], pending[2])


def _in_proj(x, g, w, cos, sin, lng, lnb, l, tm, casts=(), cast_layer=None):
    t = x.shape[0]
    steps = t // tm
    row = lambda width: pl.BlockSpec((tm, width), lambda i: (i, 0))
    widths = (GM_WIDTH, GM_WIDTH, ATTN_WIDTH, ATTN_WIDTH, ATTN_WIDTH, D_MODEL, D_MODEL)
    c_in, c_out, c_shape = _cast_plan(casts, cast_layer, steps)
    outs = pl.pallas_call(
        _with_casts(_in_proj_kernel, 7, len(widths), len(casts)),
        grid=(steps,),
        in_specs=[row(D_MODEL), _layer_spec(g, l), _layer_spec(w, None),
                  row(LANES), row(LANES), _layer_spec(lng, l), _layer_spec(lnb, l)] + c_in,
        out_specs=[row(wd) for wd in widths] + c_out,
        out_shape=[jax.ShapeDtypeStruct((t, wd), BF16) for wd in widths] + c_shape,
        compiler_params=pltpu.CompilerParams(dimension_semantics=("arbitrary",),
                                             vmem_limit_bytes=VMEM_LIMIT_BYTES),
        name="in_proj",
    )(x, g, w, cos, sin, lng, lnb, *casts)
    return outs[:len(widths)], outs[len(widths):]


def _moba_kernel(q_ref, k_ref, v_ref, o_ref, vt_ref, kbar_ref):
    nb = k_ref.shape[1] // MOBA_BLOCK
    lane = lax.broadcasted_iota(jnp.int32, (1, LANES), 1)
    key_id = lax.broadcasted_iota(jnp.int32, (MOBA_BLOCK, MOBA_BLOCK), 0)
    qry_id = lax.broadcasted_iota(jnp.int32, (MOBA_BLOCK, MOBA_BLOCK), 1)
    causal = key_id <= qry_id
    jidx = lax.broadcasted_iota(jnp.int32, (nb, MOBA_BLOCK), 0)
    drow = lax.broadcasted_iota(jnp.int32, (LANES, 1), 0)
    nt = (((1,), (1,)), ((), ()))

    vt = v_ref[0].astype(F32).T
    for hh in range(HEADS_PER_TILE):
        vt_ref[hh] = jnp.where((drow // HEAD_DIM) == hh, vt, 1.0).astype(BF16)
    for j in range(nb):
        kb = k_ref[0, j * MOBA_BLOCK:(j + 1) * MOBA_BLOCK, :]
        kbar = jnp.mean(kb.astype(F32), axis=0, keepdims=True)
        hi = kbar.astype(BF16)
        lo = (kbar - hi.astype(F32)).astype(BF16)
        kbar_ref[j:j + 1, :] = hi.astype(F32)
        kbar_ref[nb + j:nb + j + 1, :] = lo.astype(F32)
    kbar2 = kbar_ref[...].astype(BF16)

    def scores(n, hh):
        past = n * MOBA_BLOCK
        q = q_ref[0, past:past + MOBA_BLOCK, :]
        own = (lane // HEAD_DIM) == hh
        q_own = jnp.where(own, q, jnp.zeros_like(q))
        if n > MOBA_TOPK:
            g2 = lax.dot_general(kbar2, q_own, nt, preferred_element_type=F32)
            gate = jnp.where(jidx < n, g2[:nb] + g2[nb:], -jnp.inf)
            cnt = jnp.zeros((nb, MOBA_BLOCK), jnp.int32)
            for jp in range(n):
                gj = gate[jp:jp + 1, :]
                ahead = (gj > gate) | ((gj == gate) & (jp < jidx))
                cnt = cnt + jnp.where(ahead, 1, 0)
            bias = jnp.where(cnt < MOBA_TOPK, 0.0, NEG_INF)
        else:
            bias = None
        st = lax.dot_general(k_ref[0, :past + MOBA_BLOCK, :], q_own, nt,
                             preferred_element_type=F32)
        return st, bias

    def maxes(n, st, bias):
        past = n * MOBA_BLOCK
        s_blocks = [st[j * MOBA_BLOCK:(j + 1) * MOBA_BLOCK] for j in range(n)]
        s_blocks.append(jnp.where(causal, st[past:], NEG_INF))
        m = None
        for j, sj in enumerate(s_blocks):
            mj = jnp.max(sj, axis=0, keepdims=True)
            if bias is not None and j < n:
                mj = mj + bias[j:j + 1, :]
            m = mj if m is None else jnp.maximum(m, mj)
        return s_blocks, m, bias

    def probs(n, s_blocks, m, bias):
        p_blocks = []
        for j, sj in enumerate(s_blocks):
            shift = m if (bias is None or j == n) else m - bias[j:j + 1, :]
            p_blocks.append(jnp.exp2(sj - shift).astype(BF16))
        return p_blocks[0] if n == 0 else jnp.concatenate(p_blocks, axis=0)

    def output(n, hh, pt):
        acc = jnp.dot(vt_ref[hh, :, :(n + 1) * MOBA_BLOCK], pt, preferred_element_type=F32)
        ones_row = HEAD_DIM * (1 - hh)
        return acc / acc[ones_row:ones_row + 1, :]

    items = [(n, hh) for n in range(nb) for hh in range(HEADS_PER_TILE)]
    stage_a, stage_b, outs = {}, {}, {}
    for i in range(len(items) + 2):
        if i < len(items):
            stage_a[i] = maxes(items[i][0], *scores(*items[i]))
        if 0 <= i - 1 < len(items):
            stage_b[i - 1] = probs(items[i - 1][0], *stage_a.pop(i - 1))
        if 0 <= i - 2 < len(items):
            n, hh = items[i - 2]
            outs[hh] = output(n, hh, stage_b.pop(i - 2))
            if hh == HEADS_PER_TILE - 1:
                out_t = jnp.where(drow < HEAD_DIM, outs[0], outs[1])
                o_ref[0, n * MOBA_BLOCK:(n + 1) * MOBA_BLOCK, :] = out_t.T.astype(BF16)


def _moba(q, k, v, casts=(), cast_layer=None):
    b, s, w = q.shape
    tiles = w // LANES
    nb = s // MOBA_BLOCK
    steps = b * tiles
    spec = pl.BlockSpec((1, s, LANES), lambda bp: (bp // tiles, 0, bp % tiles))
    c_in, c_out, c_shape = _cast_plan(casts, cast_layer, steps)
    outs = pl.pallas_call(
        _with_casts(_moba_kernel, 3, 1, len(casts)),
        grid=(steps,),
        in_specs=[spec, spec, spec] + c_in,
        out_specs=[spec] + c_out,
        out_shape=[jax.ShapeDtypeStruct((b, s, w), BF16)] + c_shape,
        scratch_shapes=[pltpu.VMEM((HEADS_PER_TILE, LANES, s), BF16),
                        pltpu.VMEM((2 * nb, LANES), F32)],
        compiler_params=pltpu.CompilerParams(dimension_semantics=("arbitrary",),
                                             vmem_limit_bytes=VMEM_LIMIT_BYTES),
        name="moba",
    )(q, k, v, *casts)
    return outs[0], outs[1:]


def _mix_ffn_kernel(x_ref, u_ref, v_ref, yb_ref, ga_ref, gb_ref, ws_ref, bmat_ref,
                    wpa_ref, wpb_ref, wo_ref, gmix_ref, gpre_ref, wgu_ref, wd_ref, gffn_ref, o_ref):
    tm = x_ref.shape[0]
    lane = lax.broadcasted_iota(jnp.int32, (1, LANES), 1)
    first = lane < GM_GROUP_DIM
    zero = jnp.zeros((GM_CHUNK, LANES), BF16)
    wi = lax.broadcasted_iota(jnp.int32, (GM_CHUNK, 2 * GM_CHUNK), 0)
    wj = lax.broadcasted_iota(jnp.int32, (GM_CHUNK, 2 * GM_CHUNK), 1) % GM_CHUNK
    tril = wj <= wi
    bmat = bmat_ref[...]
    w_tril = [jnp.where(tril, ws_ref[gp], jnp.zeros_like(ws_ref[gp])) for gp in range(GM_WIDTH // LANES)]

    def rows_of(r):
        return slice(r * TOKEN_SUB, (r + 1) * TOKEN_SUB)

    def gating(r, st):
        n_chunks = TOKEN_SUB // GM_CHUNK
        chunk_rows = [slice(r * TOKEN_SUB + c * GM_CHUNK, r * TOKEN_SUB + (c + 1) * GM_CHUNK)
                      for c in range(n_chunks)]
        cols = []
        for gp in range(GM_WIDTH // LANES):
            stacked = []
            for rows in chunk_rows:
                vp = v_ref[rows, gp * LANES:(gp + 1) * LANES]
                stacked.append(jnp.concatenate([jnp.where(first, vp, zero), jnp.where(first, zero, vp)],
                                               axis=0))
            cols.append(jnp.dot(w_tril[gp], jnp.concatenate(stacked, axis=1), preferred_element_type=F32))
        ya = []
        for c, rows in enumerate(chunk_rows):
            mixed = jnp.concatenate([col[:, c * LANES:(c + 1) * LANES] for col in cols], axis=1) + bmat
            ya.append((u_ref[rows, :].astype(F32) * mixed).astype(BF16))
        st["ya"] = jnp.concatenate(ya, axis=0)

    def merge(r, st):
        rows = rows_of(r)
        pa = jnp.dot(st.pop("ya"), wpa_ref[...], preferred_element_type=F32)
        pb = jnp.dot(yb_ref[rows, :], wpb_ref[...], preferred_element_type=F32)
        st["merged"] = (ga_ref[rows, :].astype(F32) * pa + gb_ref[rows, :].astype(F32) * pb).astype(BF16)

    def project(r, st):
        y = jnp.dot(st.pop("merged"), wo_ref[...], preferred_element_type=F32)
        x1 = x_ref[rows_of(r), :] + _rms(y, gmix_ref[...])
        st["x1"] = x1
        st["h"] = _rms(x1, gpre_ref[...]).astype(BF16)

    def up(c):
        lo, width = FFN_CHUNKS[c]

        def stage(r, st):
            st["gate"] = jnp.dot(st["h"], wgu_ref[:, lo:lo + width], preferred_element_type=F32)
            st["lin"] = jnp.dot(st["h"], wgu_ref[:, D_FF + lo:D_FF + lo + width], preferred_element_type=F32)
        return stage

    def down(c):
        lo, width = FFN_CHUNKS[c]

        def stage(r, st):
            act = (jax.nn.silu(st.pop("gate")) * st.pop("lin")).astype(BF16)
            part = jnp.dot(act, wd_ref[lo:lo + width, :], preferred_element_type=F32)
            st["f"] = part if c == 0 else st["f"] + part
        return stage

    def finish(r, st):
        o_ref[rows_of(r), :] = st.pop("x1") + _rms(st.pop("f"), gffn_ref[...])

    stages = [gating, merge, project]
    for c in range(len(FFN_CHUNKS)):
        stages += [up(c), down(c)]
    stages.append(finish)

    state = [{} for _ in range(tm // TOKEN_SUB)]
    for stage in stages:
        for r, st in enumerate(state):
            stage(r, st)


def _mix_ffn(x, u, v, yb, ga, gb, ws, bmat, wpa, wpb, wo, gmix, gpre, wgu, wd, gffn, l, tm,
             casts=(), cast_layer=None):
    t = x.shape[0]
    steps = t // tm
    row = lambda width: pl.BlockSpec((tm, width), lambda i: (i, 0))
    params = (ws, bmat, wpa, wpb, wo, gmix, gpre, wgu, wd, gffn)
    stacked = (True, True, False, False, False, True, True, False, False, True)
    c_in, c_out, c_shape = _cast_plan(casts, cast_layer, steps)
    outs = pl.pallas_call(
        _with_casts(_mix_ffn_kernel, 6 + len(params), 1, len(casts)),
        grid=(steps,),
        in_specs=[row(D_MODEL), row(GM_WIDTH), row(GM_WIDTH), row(ATTN_WIDTH), row(D_MODEL), row(D_MODEL)]
                 + [_layer_spec(p, l if st else None) for p, st in zip(params, stacked)] + c_in,
        out_specs=[row(D_MODEL)] + c_out,
        out_shape=[jax.ShapeDtypeStruct((t, D_MODEL), F32)] + c_shape,
        compiler_params=pltpu.CompilerParams(dimension_semantics=("arbitrary",),
                                             vmem_limit_bytes=VMEM_LIMIT_BYTES),
        name="mix_ffn",
    )(x, u, v, yb, ga, gb, *params, *casts)
    return outs[0], outs[1:]


def _rope_tables(positions):
    half = HEAD_DIM // 2
    inv_freq = 1.0 / (ROPE_THETA ** (jnp.arange(0, HEAD_DIM, 2, dtype=F32) / HEAD_DIM))
    ang = positions.astype(F32).reshape(-1, 1) * inv_freq
    cos = jnp.cos(ang)
    sin = jnp.sin(ang)
    cos_t = jnp.tile(cos, (1, LANES // half))
    sin_t = jnp.tile(jnp.concatenate([-sin, sin], axis=1), (1, HEADS_PER_TILE))
    return cos_t, sin_t


def kernel(x, positions, w_in, w_s, b_s, ln_v_g, ln_v_b, w_proj_a, w_proj_b, w_out,
           g_mix_pre, g_mix_post, g_ffn_pre, g_ffn_post, w_gate_up, w_down):
    b, s, d = x.shape
    depth = w_in.shape[0]
    t = b * s
    cos_t, sin_t = _rope_tables(positions)
    xf = x.reshape(t, d)
    gains = lambda g: g.reshape(depth, 1, -1)
    ws = w_s.reshape(depth, GM_GROUPS // 2, 2, GM_CHUNK, GM_CHUNK).transpose(0, 1, 3, 2, 4)
    ws = ws.reshape(depth, GM_GROUPS // 2, GM_CHUNK, 2 * GM_CHUNK).astype(BF16)
    bmat = jnp.repeat(b_s.transpose(0, 2, 1), GM_GROUP_DIM, axis=2)
    w_in_b, wd_b, wgu_b, wpa_b, wpb_b, wo_b = (
        w[0].astype(BF16) for w in (w_in, w_down, w_gate_up, w_proj_a, w_proj_b, w_out))
    for l in range(depth):
        nxt = l + 1 if l + 1 < depth else None
        (u, v, q, k, va, ga, gb), cast_a = _in_proj(
            xf, gains(g_mix_pre), w_in_b, cos_t, sin_t, gains(ln_v_g), gains(ln_v_b), l, tm=1024,
            casts=(w_down,) if nxt else (), cast_layer=nxt)
        yb, cast_b = _moba(q.reshape(b, s, -1), k.reshape(b, s, -1), va.reshape(b, s, -1),
                           casts=(w_in, w_gate_up) if nxt else (), cast_layer=nxt)
        xf, cast_c = _mix_ffn(xf, u, v, yb.reshape(t, -1), ga, gb, ws, bmat, wpa_b, wpb_b, wo_b,
                              gains(g_mix_post), gains(g_ffn_pre), wgu_b, wd_b, gains(g_ffn_post), l, tm=512,
                              casts=(w_proj_a, w_proj_b, w_out) if nxt else (), cast_layer=nxt)
        if nxt:
            (wd_b,), (w_in_b, wgu_b), (wpa_b, wpb_b, wo_b) = cast_a, cast_b, cast_c
    return xf.reshape(b, s, d)
```

```python
import jax
import jax.numpy as jnp
from jax import lax
from jax.experimental import pallas as pl
from jax.experimental.pallas import tpu as pltpu

D_MODEL = 1024
GM_GROUPS = 8
GM_GROUP_DIM = 64
GM_WIDTH = GM_GROUPS * GM_GROUP_DIM
GM_CHUNK = 128
ATTN_HEADS = 8
HEAD_DIM = 64
ATTN_WIDTH = ATTN_HEADS * HEAD_DIM
MOBA_BLOCK = 256
MOBA_TOPK = 3
ROPE_THETA = 10000.0
D_FF = 2816
IN_WIDTH = 2 * GM_WIDTH + 3 * ATTN_WIDTH + 2 * D_MODEL
NORM_EPS = 1e-6
NEG_INF = -1e30
LOG2_E = 1.4426950408889634

QKV_LO = 2 * GM_WIDTH
GATE_LO = QKV_LO + 3 * ATTN_WIDTH
MIX_COLS = ((0, QKV_LO), (GATE_LO, IN_WIDTH))
QKV_COLS = ((QKV_LO, GATE_LO),)
MIX_WIDTH = sum(hi - lo for lo, hi in MIX_COLS)

LANES = 128
HEADS_PER_TILE = LANES // HEAD_DIM
VMEM_LIMIT_BYTES = 56 * 1024 * 1024
MXU_TILE = 256

TOKEN_SUB = 256
FFN_CHUNKS = ((0, 5 * MXU_TILE), (5 * MXU_TILE, D_FF - 5 * MXU_TILE))

F32 = jnp.float32
BF16 = jnp.bfloat16


def _rms(x, g):
    return x * lax.rsqrt(jnp.mean(x * x, axis=-1, keepdims=True) + NORM_EPS) * g


def _layer_spec(param, l):
    if l is None:
        return pl.BlockSpec(param.shape, lambda *_: (0,) * param.ndim, pipeline_mode=pl.Buffered(1))
    shape = param.shape[1:]
    return pl.BlockSpec((None,) + shape, lambda *_: (l,) + (0,) * len(shape),
                        pipeline_mode=pl.Buffered(1))


def _cast_plan(items, steps):
    in_specs, out_specs, out_shapes, dests_per_item = [], [], [], []
    for w, layer, dests in items:
        _, rows, cols = w.shape
        rb = rows // steps
        in_specs.append(pl.BlockSpec((None, rb, cols), lambda i, layer=layer: (layer, i, 0)))
        dests = tuple(((0, cols),) if d is None else d for d in dests)
        for d in dests:
            width = sum(hi - lo for lo, hi in d)
            out_specs.append(pl.BlockSpec((rb, width), lambda i: (i, 0)))
            out_shapes.append(jax.ShapeDtypeStruct((rows, width), BF16))
        dests_per_item.append(dests)
    return in_specs, out_specs, out_shapes, dests_per_item


def _with_casts(body, n_in, n_out, dests_per_item):
    n_src = len(dests_per_item)
    n_dst = sum(len(d) for d in dests_per_item)

    def kernel_fn(*refs):
        ins, rest = refs[:n_in], refs[n_in:]
        srcs, rest = rest[:n_src], rest[n_src:]
        outs, rest = rest[:n_out], rest[n_out:]
        dsts, scratch = list(rest[:n_dst]), rest[n_dst:]
        for src, dests in zip(srcs, dests_per_item):
            for ranges in dests:
                dst = dsts.pop(0)
                off = 0
                for lo, hi in ranges:
                    dst[:, off:off + hi - lo] = src[:, lo:hi].astype(BF16)
                    off += hi - lo
        body(*ins, *outs, *scratch)
    return kernel_fn


def _qkv_proj_kernel(x_ref, g_ref, w_ref, cos_ref, sin_ref, q_ref, k_ref, va_ref):
    tm = x_ref.shape[0]
    first_half = (lax.broadcasted_iota(jnp.int32, (TOKEN_SUB, LANES), 1) % HEAD_DIM) < HEAD_DIM // 2

    def rope_out(out_ref, scale):
        def ep(z, rows):
            cos = cos_ref[rows, :]
            sin = sin_ref[rows, :]
            for c in range(ATTN_WIDTH // LANES):
                zc = z[:, c * LANES:(c + 1) * LANES]
                partner = jnp.where(first_half,
                                    pltpu.roll(zc, LANES - HEAD_DIM // 2, 1),
                                    pltpu.roll(zc, HEAD_DIM // 2, 1))
                r = zc * cos + partner * sin
                if scale != 1.0:
                    r = r * scale
                out_ref[rows, c * LANES:(c + 1) * LANES] = r.astype(BF16)
        return ep

    def cast_out(z, rows):
        va_ref[rows, :] = z.astype(BF16)

    sections = [
        (0, rope_out(q_ref, HEAD_DIM ** -0.5 * LOG2_E)),
        (ATTN_WIDTH, rope_out(k_ref, 1.0)),
        (2 * ATTN_WIDTH, cast_out),
    ]
    items = [(r, sec) for r in range(tm // TOKEN_SUB) for sec in sections]

    h = {}
    pending = None
    for r, (lo, ep) in items:
        rows = slice(r * TOKEN_SUB, (r + 1) * TOKEN_SUB)
        if r not in h:
            h[r] = _rms(x_ref[rows, :], g_ref[...]).astype(BF16)
        z = jnp.dot(h[r], w_ref[:, lo:lo + ATTN_WIDTH], preferred_element_type=F32)
        if pending is not None:
            pending[0](pending[1], pending[2])
        pending = (ep, z, rows)
    pending[0](pending[1], pending[2])


def _qkv_proj(x, g, w_qkv, cos, sin, l, tm):
    t = x.shape[0]
    row = lambda width: pl.BlockSpec((tm, width), lambda i: (i, 0))
    return pl.pallas_call(
        _qkv_proj_kernel,
        grid=(t // tm,),
        in_specs=[row(D_MODEL), _layer_spec(g, l), _layer_spec(w_qkv, None), row(LANES), row(LANES)],
        out_specs=[row(ATTN_WIDTH)] * 3,
        out_shape=[jax.ShapeDtypeStruct((t, ATTN_WIDTH), BF16)] * 3,
        compiler_params=pltpu.CompilerParams(dimension_semantics=("arbitrary",),
                                             vmem_limit_bytes=VMEM_LIMIT_BYTES),
        name="qkv_proj",
    )(x, g, w_qkv, cos, sin)


def _moba_kernel(q_ref, k_ref, v_ref, o_ref, vt_ref, kbar_ref):
    nb = k_ref.shape[1] // MOBA_BLOCK
    lane = lax.broadcasted_iota(jnp.int32, (1, LANES), 1)
    key_id = lax.broadcasted_iota(jnp.int32, (MOBA_BLOCK, MOBA_BLOCK), 0)
    qry_id = lax.broadcasted_iota(jnp.int32, (MOBA_BLOCK, MOBA_BLOCK), 1)
    causal = key_id <= qry_id
    jidx = lax.broadcasted_iota(jnp.int32, (nb, MOBA_BLOCK), 0)
    drow = lax.broadcasted_iota(jnp.int32, (LANES, 1), 0)
    nt = (((1,), (1,)), ((), ()))

    vt = v_ref[0].astype(F32).T
    for hh in range(HEADS_PER_TILE):
        vt_ref[hh] = jnp.where((drow // HEAD_DIM) == hh, vt, 1.0).astype(BF16)
    for j in range(nb):
        kb = k_ref[0, j * MOBA_BLOCK:(j + 1) * MOBA_BLOCK, :]
        kbar = jnp.mean(kb.astype(F32), axis=0, keepdims=True)
        hi = kbar.astype(BF16)
        lo = (kbar - hi.astype(F32)).astype(BF16)
        kbar_ref[j:j + 1, :] = hi.astype(F32)
        kbar_ref[nb + j:nb + j + 1, :] = lo.astype(F32)
    kbar2 = kbar_ref[...].astype(BF16)

    def scores(n, hh):
        past = n * MOBA_BLOCK
        q = q_ref[0, past:past + MOBA_BLOCK, :]
        own = (lane // HEAD_DIM) == hh
        q_own = jnp.where(own, q, jnp.zeros_like(q))
        if n > MOBA_TOPK:
            g2 = lax.dot_general(kbar2, q_own, nt, preferred_element_type=F32)
            gate = jnp.where(jidx < n, g2[:nb] + g2[nb:], -jnp.inf)
            cnt = jnp.zeros((nb, MOBA_BLOCK), jnp.int32)
            for jp in range(n):
                gj = gate[jp:jp + 1, :]
                ahead = (gj > gate) | ((gj == gate) & (jp < jidx))
                cnt = cnt + jnp.where(ahead, 1, 0)
            bias = jnp.where(cnt < MOBA_TOPK, 0.0, NEG_INF)
        else:
            bias = None
        st = lax.dot_general(k_ref[0, :past + MOBA_BLOCK, :], q_own, nt,
                             preferred_element_type=F32)
        return st, bias

    def maxes(n, st, bias):
        past = n * MOBA_BLOCK
        s_blocks = [st[j * MOBA_BLOCK:(j + 1) * MOBA_BLOCK] for j in range(n)]
        s_blocks.append(jnp.where(causal, st[past:], NEG_INF))
        m = None
        for j, sj in enumerate(s_blocks):
            mj = jnp.max(sj, axis=0, keepdims=True)
            if bias is not None and j < n:
                mj = mj + bias[j:j + 1, :]
            m = mj if m is None else jnp.maximum(m, mj)
        return s_blocks, m, bias

    def probs(n, s_blocks, m, bias):
        p_blocks = []
        for j, sj in enumerate(s_blocks):
            shift = m if (bias is None or j == n) else m - bias[j:j + 1, :]
            p_blocks.append(jnp.exp2(sj - shift).astype(BF16))
        return p_blocks[0] if n == 0 else jnp.concatenate(p_blocks, axis=0)

    def output(n, hh, pt):
        acc = jnp.dot(vt_ref[hh, :, :(n + 1) * MOBA_BLOCK], pt, preferred_element_type=F32)
        ones_row = HEAD_DIM * (1 - hh)
        return acc / acc[ones_row:ones_row + 1, :]

    items = [(n, hh) for n in reversed(range(nb)) for hh in range(HEADS_PER_TILE)]
    stage_a, stage_b, outs = {}, {}, {}
    for i in range(len(items) + 2):
        if i < len(items):
            stage_a[i] = maxes(items[i][0], *scores(*items[i]))
        if 0 <= i - 1 < len(items):
            stage_b[i - 1] = probs(items[i - 1][0], *stage_a.pop(i - 1))
        if 0 <= i - 2 < len(items):
            n, hh = items[i - 2]
            outs[hh] = output(n, hh, stage_b.pop(i - 2))
            if hh == HEADS_PER_TILE - 1:
                out_t = jnp.where(drow < HEAD_DIM, outs[0], outs[1])
                o_ref[0, n * MOBA_BLOCK:(n + 1) * MOBA_BLOCK, :] = out_t.T.astype(BF16)


def _moba(q, k, v, casts=()):
    b, s, w = q.shape
    tiles = w // LANES
    nb = s // MOBA_BLOCK
    steps = b * tiles
    spec = pl.BlockSpec((1, s, LANES), lambda bp: (bp // tiles, 0, bp % tiles))
    c_in, c_out, c_shape, dests = _cast_plan(casts, steps)
    outs = pl.pallas_call(
        _with_casts(_moba_kernel, 3, 1, dests),
        grid=(steps,),
        in_specs=[spec, spec, spec] + c_in,
        out_specs=[spec] + c_out,
        out_shape=[jax.ShapeDtypeStruct((b, s, w), BF16)] + c_shape,
        scratch_shapes=[pltpu.VMEM((HEADS_PER_TILE, LANES, s), BF16),
                        pltpu.VMEM((2 * nb, LANES), F32)],
        compiler_params=pltpu.CompilerParams(dimension_semantics=("arbitrary",),
                                             vmem_limit_bytes=VMEM_LIMIT_BYTES),
        name="moba",
    )(q, k, v, *[c[0] for c in casts])
    return outs[0], outs[1:]


def _mix_ffn_kernel(x_ref, yb_ref, gin_ref, wmix_ref, lng_ref, lnb_ref, ws_ref, bmat_ref,
                    wpa_ref, wpb_ref, wo_ref, gmix_ref, gpre_ref, wgu_ref, wd_ref, gffn_ref, o_ref):
    tm = x_ref.shape[0]
    lane = lax.broadcasted_iota(jnp.int32, (1, LANES), 1)
    first = lane < GM_GROUP_DIM
    zero = jnp.zeros((GM_CHUNK, LANES), BF16)
    wi = lax.broadcasted_iota(jnp.int32, (GM_CHUNK, 2 * GM_CHUNK), 0)
    wj = lax.broadcasted_iota(jnp.int32, (GM_CHUNK, 2 * GM_CHUNK), 1) % GM_CHUNK
    tril = wj <= wi
    bmat = bmat_ref[...]
    w_tril = [jnp.where(tril, ws_ref[gp], jnp.zeros_like(ws_ref[gp])) for gp in range(GM_WIDTH // LANES)]
    n_chunks = TOKEN_SUB // GM_CHUNK

    def rows_of(r):
        return slice(r * TOKEN_SUB, (r + 1) * TOKEN_SUB)

    def proj(key, lo, width):
        def stage(r, st):
            if "h0" not in st:
                st["h0"] = _rms(x_ref[rows_of(r), :], gin_ref[...]).astype(BF16)
            st[key] = jnp.dot(st["h0"], wmix_ref[:, lo:lo + width], preferred_element_type=F32)
        return stage

    def gelu_u(r, st):
        st["u"] = jax.nn.gelu(st.pop("zu"))

    def gelu_ln_v(r, st):
        v = jax.nn.gelu(st.pop("zv"))
        mu = jnp.mean(v, axis=-1, keepdims=True)
        vc = v - mu
        var = jnp.mean(vc * vc, axis=-1, keepdims=True)
        st["v"] = (vc * lax.rsqrt(var + NORM_EPS) * lng_ref[...] + lnb_ref[...]).astype(BF16)

    def gating(r, st):
        v = st.pop("v")
        u = st.pop("u")
        cols = []
        for gp in range(GM_WIDTH // LANES):
            stacked = []
            for c in range(n_chunks):
                vp = v[c * GM_CHUNK:(c + 1) * GM_CHUNK, gp * LANES:(gp + 1) * LANES]
                stacked.append(jnp.concatenate([jnp.where(first, vp, zero), jnp.where(first, zero, vp)],
                                               axis=0))
            cols.append(jnp.dot(w_tril[gp], jnp.concatenate(stacked, axis=1), preferred_element_type=F32))
        ya = []
        for c in range(n_chunks):
            mixed = jnp.concatenate([col[:, c * LANES:(c + 1) * LANES] for col in cols], axis=1) + bmat
            ya.append((u[c * GM_CHUNK:(c + 1) * GM_CHUNK] * mixed).astype(BF16))
        st["ya"] = jnp.concatenate(ya, axis=0)

    def branch(key, src, w_ref, gate_key):
        def stage(r, st):
            operand = st.pop("ya") if src == "ya" else yb_ref[rows_of(r), :]
            p = jnp.dot(operand, w_ref[...], preferred_element_type=F32)
            st[key] = jax.nn.sigmoid(st.pop(gate_key)) * p
        return stage

    def project(r, st):
        merged = (st.pop("ma") + st.pop("mb")).astype(BF16)
        st.pop("h0")
        y = jnp.dot(merged, wo_ref[...], preferred_element_type=F32)
        x1 = x_ref[rows_of(r), :] + _rms(y, gmix_ref[...])
        st["x1"] = x1
        st["h"] = _rms(x1, gpre_ref[...]).astype(BF16)

    def up(c):
        lo, width = FFN_CHUNKS[c]

        def stage(r, st):
            st["gate"] = jnp.dot(st["h"], wgu_ref[:, lo:lo + width], preferred_element_type=F32)
            st["lin"] = jnp.dot(st["h"], wgu_ref[:, D_FF + lo:D_FF + lo + width], preferred_element_type=F32)
        return stage

    def down(c):
        lo, width = FFN_CHUNKS[c]

        def stage(r, st):
            act = (jax.nn.silu(st.pop("gate")) * st.pop("lin")).astype(BF16)
            part = jnp.dot(act, wd_ref[lo:lo + width, :], preferred_element_type=F32)
            st["f"] = part if c == 0 else st["f"] + part
        return stage

    def finish(r, st):
        o_ref[rows_of(r), :] = st.pop("x1") + _rms(st.pop("f"), gffn_ref[...])

    stages = [proj("zu", 0, GM_WIDTH), proj("zv", GM_WIDTH, GM_WIDTH), gelu_u, gelu_ln_v,
              proj("zga", 2 * GM_WIDTH, D_MODEL), gating, proj("zgb", 2 * GM_WIDTH + D_MODEL, D_MODEL),
              branch("ma", "ya", wpa_ref, "zga"), branch("mb", "yb", wpb_ref, "zgb"), project]
    for c in range(len(FFN_CHUNKS)):
        stages += [up(c), down(c)]
    stages.append(finish)

    state = [{} for _ in range(tm // TOKEN_SUB)]
    for stage in stages:
        for r, st in enumerate(state):
            stage(r, st)


def _mix_ffn(x, yb, params, l, tm, casts=()):
    t = x.shape[0]
    steps = t // tm
    row = lambda width: pl.BlockSpec((tm, width), lambda i: (i, 0))
    c_in, c_out, c_shape, dests = _cast_plan(casts, steps)
    outs = pl.pallas_call(
        _with_casts(_mix_ffn_kernel, 2 + len(params), 1, dests),
        grid=(steps,),
        in_specs=[row(D_MODEL), row(ATTN_WIDTH)]
                 + [_layer_spec(p, l if stacked else None) for p, stacked in params] + c_in,
        out_specs=[row(D_MODEL)] + c_out,
        out_shape=[jax.ShapeDtypeStruct((t, D_MODEL), F32)] + c_shape,
        compiler_params=pltpu.CompilerParams(dimension_semantics=("arbitrary",),
                                             vmem_limit_bytes=VMEM_LIMIT_BYTES),
        name="mix_ffn",
    )(x, yb, *[p for p, _ in params], *[c[0] for c in casts])
    return outs[0], outs[1:]


def _rope_tables(positions):
    half = HEAD_DIM // 2
    inv_freq = 1.0 / (ROPE_THETA ** (jnp.arange(0, HEAD_DIM, 2, dtype=F32) / HEAD_DIM))
    ang = positions.astype(F32).reshape(-1, 1) * inv_freq
    cos = jnp.cos(ang)
    sin = jnp.sin(ang)
    cos_t = jnp.tile(cos, (1, LANES // half))
    sin_t = jnp.tile(jnp.concatenate([-sin, sin], axis=1), (1, HEADS_PER_TILE))
    return cos_t, sin_t


def kernel(x, positions, w_in, w_s, b_s, ln_v_g, ln_v_b, w_proj_a, w_proj_b, w_out,
           g_mix_pre, g_mix_post, g_ffn_pre, g_ffn_post, w_gate_up, w_down):
    b, s, d = x.shape
    depth = w_in.shape[0]
    t = b * s
    cos_t, sin_t = _rope_tables(positions)
    xf = x.reshape(t, d)
    gains = lambda g: g.reshape(depth, 1, -1)
    ws = w_s.reshape(depth, GM_GROUPS // 2, 2, GM_CHUNK, GM_CHUNK).transpose(0, 1, 3, 2, 4)
    ws = ws.reshape(depth, GM_GROUPS // 2, GM_CHUNK, 2 * GM_CHUNK).astype(BF16)
    bmat = jnp.repeat(b_s.transpose(0, 2, 1), GM_GROUP_DIM, axis=2)
    w_down_v = w_down.reshape(depth, -1, D_MODEL // 2)

    def matmul_weight_casts(layer, with_qkv):
        in_dests = (MIX_COLS, QKV_COLS) if with_qkv else (MIX_COLS,)
        return [(w_in, layer, in_dests), (w_gate_up, layer, (None,)), (w_down_v, layer, (None,)),
                (w_proj_a, layer, (None,)), (w_proj_b, layer, (None,)), (w_out, layer, (None,))]

    def unpack(arrays, with_qkv):
        arrays = list(arrays)
        w = {"mix": arrays.pop(0)}
        if with_qkv:
            w["qkv"] = arrays.pop(0)
        w["gu"], wd, w["pa"], w["pb"], w["o"] = arrays
        w["d"] = wd.reshape(D_FF, D_MODEL)
        return w

    weights = {0: {"qkv": w_in[0][:, QKV_LO:GATE_LO].astype(BF16)}}
    for l in range(depth):
        q, k, va = _qkv_proj(xf, gains(g_mix_pre), weights[l]["qkv"], cos_t, sin_t, l, tm=1024)
        casts = matmul_weight_casts(0, False) if l == 0 else []
        if l + 1 < depth:
            casts += matmul_weight_casts(l + 1, True)
        yb, cast_out = _moba(q.reshape(b, s, -1), k.reshape(b, s, -1), va.reshape(b, s, -1), casts)
        cast_out = list(cast_out)
        if l == 0:
            weights[0].update(unpack(cast_out[:6], False))
            cast_out = cast_out[6:]
        if l + 1 < depth:
            weights[l + 1] = unpack(cast_out, True)
        w = weights.pop(l)
        params = [(gains(g_mix_pre), True), (w["mix"], False), (gains(ln_v_g), True), (gains(ln_v_b), True),
                  (ws, True), (bmat, True), (w["pa"], False), (w["pb"], False), (w["o"], False),
                  (gains(g_mix_post), True), (gains(g_ffn_pre), True), (w["gu"], False), (w["d"], False),
                  (gains(g_ffn_post), True)]
        xf, _ = _mix_ffn(xf, yb.reshape(t, -1), params, l, tm=512)
    return xf.reshape(b, s, d)
```

```python
import jax
import jax.numpy as jnp
from jax import lax
from jax.experimental import pallas as pl
from jax.experimental.pallas import tpu as pltpu

D_MODEL = 1024
GM_GROUPS = 8
GM_GROUP_DIM = 64
GM_WIDTH = GM_GROUPS * GM_GROUP_DIM
GM_CHUNK = 128
ATTN_HEADS = 8
HEAD_DIM = 64
ATTN_WIDTH = ATTN_HEADS * HEAD_DIM
MOBA_BLOCK = 256
MOBA_TOPK = 3
ROPE_THETA = 10000.0
D_FF = 2816
IN_WIDTH = 2 * GM_WIDTH + 3 * ATTN_WIDTH + 2 * D_MODEL
NORM_EPS = 1e-6
NEG_INF = -1e30
LOG2_E = 1.4426950408889634

QKV_LO = 2 * GM_WIDTH
GATE_LO = QKV_LO + 3 * ATTN_WIDTH
MIX_COLS = ((0, QKV_LO), (GATE_LO, IN_WIDTH))
QKV_COLS = ((QKV_LO, GATE_LO),)

LANES = 128
HEADS_PER_TILE = LANES // HEAD_DIM
VMEM_LIMIT_BYTES = 56 * 1024 * 1024
MXU_TILE = 256

TOKEN_SUB = 256
FFN_CHUNKS = ((0, 5 * MXU_TILE), (5 * MXU_TILE, D_FF - 5 * MXU_TILE))

F32 = jnp.float32
BF16 = jnp.bfloat16


def _rms(x, g):
    return x * lax.rsqrt(jnp.mean(x * x, axis=-1, keepdims=True) + NORM_EPS) * g


def _layer_spec(param, l):
    if l is None:
        return pl.BlockSpec(param.shape, lambda *_: (0,) * param.ndim, pipeline_mode=pl.Buffered(1))
    shape = param.shape[1:]
    return pl.BlockSpec((None,) + shape, lambda *_: (l,) + (0,) * len(shape),
                        pipeline_mode=pl.Buffered(1))


def _cast_plan(items, steps):
    in_specs, out_specs, out_shapes, dests_per_item = [], [], [], []
    for w, layer, dests in items:
        _, rows, cols = w.shape
        rb = rows // steps
        in_specs.append(pl.BlockSpec((None, rb, cols), lambda i, layer=layer: (layer, i, 0)))
        dests = tuple(((0, cols),) if d is None else d for d in dests)
        for d in dests:
            width = sum(hi - lo for lo, hi in d)
            out_specs.append(pl.BlockSpec((rb, width), lambda i: (i, 0)))
            out_shapes.append(jax.ShapeDtypeStruct((rows, width), BF16))
        dests_per_item.append(dests)
    return in_specs, out_specs, out_shapes, dests_per_item


def _with_casts(body, n_in, n_out, dests_per_item):
    n_src = len(dests_per_item)
    n_dst = sum(len(d) for d in dests_per_item)

    def kernel_fn(*refs):
        ins, rest = refs[:n_in], refs[n_in:]
        srcs, rest = rest[:n_src], rest[n_src:]
        outs, rest = rest[:n_out], rest[n_out:]
        dsts, scratch = list(rest[:n_dst]), rest[n_dst:]
        for src, dests in zip(srcs, dests_per_item):
            for ranges in dests:
                dst = dsts.pop(0)
                off = 0
                for lo, hi in ranges:
                    dst[:, off:off + hi - lo] = src[:, lo:hi].astype(BF16)
                    off += hi - lo
        body(*ins, *outs, *scratch)
    return kernel_fn


def _qkv_proj_kernel(x_ref, g_ref, w_ref, cos_ref, sin_ref, q_ref, k_ref, va_ref):
    tm = x_ref.shape[0]
    first_half = (lax.broadcasted_iota(jnp.int32, (TOKEN_SUB, LANES), 1) % HEAD_DIM) < HEAD_DIM // 2

    def rope_out(out_ref, scale):
        def ep(z, rows):
            cos = cos_ref[rows, :]
            sin = sin_ref[rows, :]
            for c in range(ATTN_WIDTH // LANES):
                zc = z[:, c * LANES:(c + 1) * LANES]
                partner = jnp.where(first_half,
                                    pltpu.roll(zc, LANES - HEAD_DIM // 2, 1),
                                    pltpu.roll(zc, HEAD_DIM // 2, 1))
                r = zc * cos + partner * sin
                if scale != 1.0:
                    r = r * scale
                out_ref[rows, c * LANES:(c + 1) * LANES] = r.astype(BF16)
        return ep

    def cast_out(z, rows):
        va_ref[rows, :] = z.astype(BF16)

    sections = [
        (0, rope_out(q_ref, HEAD_DIM ** -0.5 * LOG2_E)),
        (ATTN_WIDTH, rope_out(k_ref, 1.0)),
        (2 * ATTN_WIDTH, cast_out),
    ]
    items = [(r, sec) for r in range(tm // TOKEN_SUB) for sec in sections]

    h = {}
    pending = None
    for r, (lo, ep) in items:
        rows = slice(r * TOKEN_SUB, (r + 1) * TOKEN_SUB)
        if r not in h:
            h[r] = _rms(x_ref[rows, :], g_ref[...]).astype(BF16)
        z = jnp.dot(h[r], w_ref[:, lo:lo + ATTN_WIDTH], preferred_element_type=F32)
        if pending is not None:
            pending[0](pending[1], pending[2])
        pending = (ep, z, rows)
    pending[0](pending[1], pending[2])


def _qkv_proj(x, g, w_qkv, cos, sin, l, tm, casts=()):
    t = x.shape[0]
    steps = t // tm
    row = lambda width: pl.BlockSpec((tm, width), lambda i: (i, 0))
    c_in, c_out, c_shape, dests = _cast_plan(casts, steps)
    outs = pl.pallas_call(
        _with_casts(_qkv_proj_kernel, 5, 3, dests),
        grid=(steps,),
        in_specs=[row(D_MODEL), _layer_spec(g, l), _layer_spec(w_qkv, None), row(LANES), row(LANES)] + c_in,
        out_specs=[row(ATTN_WIDTH)] * 3 + c_out,
        out_shape=[jax.ShapeDtypeStruct((t, ATTN_WIDTH), BF16)] * 3 + c_shape,
        compiler_params=pltpu.CompilerParams(dimension_semantics=("arbitrary",),
                                             vmem_limit_bytes=VMEM_LIMIT_BYTES),
        name="qkv_proj",
    )(x, g, w_qkv, cos, sin, *[c[0] for c in casts])
    return outs[:3], outs[3:]


def _moba_kernel(q_ref, k_ref, v_ref, o_ref, vt_ref, kbar_ref):
    nb = k_ref.shape[1] // MOBA_BLOCK
    lane = lax.broadcasted_iota(jnp.int32, (1, LANES), 1)
    key_id = lax.broadcasted_iota(jnp.int32, (MOBA_BLOCK, MOBA_BLOCK), 0)
    qry_id = lax.broadcasted_iota(jnp.int32, (MOBA_BLOCK, MOBA_BLOCK), 1)
    causal = key_id <= qry_id
    jidx = lax.broadcasted_iota(jnp.int32, (nb, MOBA_BLOCK), 0)
    drow = lax.broadcasted_iota(jnp.int32, (LANES, 1), 0)
    nt = (((1,), (1,)), ((), ()))

    vt = v_ref[0].astype(F32).T
    for hh in range(HEADS_PER_TILE):
        vt_ref[hh] = jnp.where((drow // HEAD_DIM) == hh, vt, 1.0).astype(BF16)
    for j in range(nb):
        kb = k_ref[0, j * MOBA_BLOCK:(j + 1) * MOBA_BLOCK, :]
        kbar = jnp.mean(kb.astype(F32), axis=0, keepdims=True)
        hi = kbar.astype(BF16)
        lo = (kbar - hi.astype(F32)).astype(BF16)
        kbar_ref[j:j + 1, :] = hi.astype(F32)
        kbar_ref[nb + j:nb + j + 1, :] = lo.astype(F32)
    kbar2 = kbar_ref[...].astype(BF16)

    def scores(n, hh):
        past = n * MOBA_BLOCK
        q = q_ref[0, past:past + MOBA_BLOCK, :]
        own = (lane // HEAD_DIM) == hh
        q_own = jnp.where(own, q, jnp.zeros_like(q))
        if n > MOBA_TOPK:
            g2 = lax.dot_general(kbar2, q_own, nt, preferred_element_type=F32)
            gate = jnp.where(jidx < n, g2[:nb] + g2[nb:], -jnp.inf)
            cnt = jnp.zeros((nb, MOBA_BLOCK), jnp.int32)
            for jp in range(n):
                gj = gate[jp:jp + 1, :]
                ahead = (gj > gate) | ((gj == gate) & (jp < jidx))
                cnt = cnt + jnp.where(ahead, 1, 0)
            bias = jnp.where(cnt < MOBA_TOPK, 0.0, NEG_INF)
        else:
            bias = None
        st = lax.dot_general(k_ref[0, :past + MOBA_BLOCK, :], q_own, nt,
                             preferred_element_type=F32)
        return st, bias

    def maxes(n, st, bias):
        past = n * MOBA_BLOCK
        s_blocks = [st[j * MOBA_BLOCK:(j + 1) * MOBA_BLOCK] for j in range(n)]
        s_blocks.append(jnp.where(causal, st[past:], NEG_INF))
        m = None
        for j, sj in enumerate(s_blocks):
            mj = jnp.max(sj, axis=0, keepdims=True)
            if bias is not None and j < n:
                mj = mj + bias[j:j + 1, :]
            m = mj if m is None else jnp.maximum(m, mj)
        return s_blocks, m, bias

    def probs(n, s_blocks, m, bias):
        p_blocks = []
        for j, sj in enumerate(s_blocks):
            shift = m if (bias is None or j == n) else m - bias[j:j + 1, :]
            p_blocks.append(jnp.exp2(sj - shift).astype(BF16))
        return p_blocks[0] if n == 0 else jnp.concatenate(p_blocks, axis=0)

    def output(n, hh, pt):
        acc = jnp.dot(vt_ref[hh, :, :(n + 1) * MOBA_BLOCK], pt, preferred_element_type=F32)
        ones_row = HEAD_DIM * (1 - hh)
        return acc / acc[ones_row:ones_row + 1, :]

    items = [(n, hh) for n in reversed(range(nb)) for hh in range(HEADS_PER_TILE)]
    stage_a, stage_b, outs = {}, {}, {}
    for i in range(len(items) + 2):
        if i < len(items):
            stage_a[i] = maxes(items[i][0], *scores(*items[i]))
        if 0 <= i - 1 < len(items):
            stage_b[i - 1] = probs(items[i - 1][0], *stage_a.pop(i - 1))
        if 0 <= i - 2 < len(items):
            n, hh = items[i - 2]
            outs[hh] = output(n, hh, stage_b.pop(i - 2))
            if hh == HEADS_PER_TILE - 1:
                out_t = jnp.where(drow < HEAD_DIM, outs[0], outs[1])
                o_ref[0, n * MOBA_BLOCK:(n + 1) * MOBA_BLOCK, :] = out_t.T.astype(BF16)


def _moba(q, k, v, casts=()):
    b, s, w = q.shape
    tiles = w // LANES
    nb = s // MOBA_BLOCK
    steps = b * tiles
    spec = pl.BlockSpec((1, s, LANES), lambda bp: (bp // tiles, 0, bp % tiles))
    c_in, c_out, c_shape, dests = _cast_plan(casts, steps)
    outs = pl.pallas_call(
        _with_casts(_moba_kernel, 3, 1, dests),
        grid=(steps,),
        in_specs=[spec, spec, spec] + c_in,
        out_specs=[spec] + c_out,
        out_shape=[jax.ShapeDtypeStruct((b, s, w), BF16)] + c_shape,
        scratch_shapes=[pltpu.VMEM((HEADS_PER_TILE, LANES, s), BF16),
                        pltpu.VMEM((2 * nb, LANES), F32)],
        compiler_params=pltpu.CompilerParams(dimension_semantics=("arbitrary",),
                                             vmem_limit_bytes=VMEM_LIMIT_BYTES),
        name="moba",
    )(q, k, v, *[c[0] for c in casts])
    return outs[0], outs[1:]


def _mix_ffn_kernel(x_ref, yb_ref, gin_ref, wmix_ref, lng_ref, lnb_ref, ws_ref, bmat_ref,
                    wpa_ref, wpb_ref, wo_ref, gmix_ref, gpre_ref, wgu_ref, wd_ref, gffn_ref, o_ref):
    tm = x_ref.shape[0]
    lane = lax.broadcasted_iota(jnp.int32, (1, LANES), 1)
    first = lane < GM_GROUP_DIM
    zero = jnp.zeros((GM_CHUNK, LANES), BF16)
    wi = lax.broadcasted_iota(jnp.int32, (GM_CHUNK, 2 * GM_CHUNK), 0)
    wj = lax.broadcasted_iota(jnp.int32, (GM_CHUNK, 2 * GM_CHUNK), 1) % GM_CHUNK
    tril = wj <= wi
    bmat = bmat_ref[...]
    w_tril = [jnp.where(tril, ws_ref[gp], jnp.zeros_like(ws_ref[gp])) for gp in range(GM_WIDTH // LANES)]
    n_chunks = TOKEN_SUB // GM_CHUNK

    def rows_of(r):
        return slice(r * TOKEN_SUB, (r + 1) * TOKEN_SUB)

    def proj(key, lo, width):
        def stage(r, st):
            if "h0" not in st:
                st["h0"] = _rms(x_ref[rows_of(r), :], gin_ref[...]).astype(BF16)
            st[key] = jnp.dot(st["h0"], wmix_ref[:, lo:lo + width], preferred_element_type=F32)
        return stage

    def gelu_u(r, st):
        st["u"] = jax.nn.gelu(st.pop("zu"))

    def gelu_ln_v(r, st):
        v = jax.nn.gelu(st.pop("zv"))
        mu = jnp.mean(v, axis=-1, keepdims=True)
        vc = v - mu
        var = jnp.mean(vc * vc, axis=-1, keepdims=True)
        st["v"] = (vc * lax.rsqrt(var + NORM_EPS) * lng_ref[...] + lnb_ref[...]).astype(BF16)

    def gating(r, st):
        v = st.pop("v")
        u = st.pop("u")
        cols = []
        for gp in range(GM_WIDTH // LANES):
            stacked = []
            for c in range(n_chunks):
                vp = v[c * GM_CHUNK:(c + 1) * GM_CHUNK, gp * LANES:(gp + 1) * LANES]
                stacked.append(jnp.concatenate([jnp.where(first, vp, zero), jnp.where(first, zero, vp)],
                                               axis=0))
            cols.append(jnp.dot(w_tril[gp], jnp.concatenate(stacked, axis=1), preferred_element_type=F32))
        ya = []
        for c in range(n_chunks):
            mixed = jnp.concatenate([col[:, c * LANES:(c + 1) * LANES] for col in cols], axis=1) + bmat
            ya.append((u[c * GM_CHUNK:(c + 1) * GM_CHUNK] * mixed).astype(BF16))
        st["ya"] = jnp.concatenate(ya, axis=0)

    def branch(key, src, w_ref, gate_key):
        def stage(r, st):
            operand = st.pop("ya") if src == "ya" else yb_ref[rows_of(r), :]
            p = jnp.dot(operand, w_ref[...], preferred_element_type=F32)
            st[key] = jax.nn.sigmoid(st.pop(gate_key)) * p
        return stage

    def project(r, st):
        merged = (st.pop("ma") + st.pop("mb")).astype(BF16)
        st.pop("h0")
        y = jnp.dot(merged, wo_ref[...], preferred_element_type=F32)
        x1 = x_ref[rows_of(r), :] + _rms(y, gmix_ref[...])
        st["x1"] = x1
        st["h"] = _rms(x1, gpre_ref[...]).astype(BF16)

    def up(c):
        lo, width = FFN_CHUNKS[c]

        def stage(r, st):
            st["gate"] = jnp.dot(st["h"], wgu_ref[:, lo:lo + width], preferred_element_type=F32)
            st["lin"] = jnp.dot(st["h"], wgu_ref[:, D_FF + lo:D_FF + lo + width], preferred_element_type=F32)
        return stage

    def down(c):
        lo, width = FFN_CHUNKS[c]

        def stage(r, st):
            act = (jax.nn.silu(st.pop("gate")) * st.pop("lin")).astype(BF16)
            part = jnp.dot(act, wd_ref[lo:lo + width, :], preferred_element_type=F32)
            st["f"] = part if c == 0 else st["f"] + part
        return stage

    def finish(r, st):
        o_ref[rows_of(r), :] = st.pop("x1") + _rms(st.pop("f"), gffn_ref[...])

    stages = [proj("zu", 0, GM_WIDTH), proj("zv", GM_WIDTH, GM_WIDTH), gelu_u, gelu_ln_v,
              proj("zga", 2 * GM_WIDTH, D_MODEL), gating, proj("zgb", 2 * GM_WIDTH + D_MODEL, D_MODEL),
              branch("ma", "ya", wpa_ref, "zga"), branch("mb", "yb", wpb_ref, "zgb"), project]
    for c in range(len(FFN_CHUNKS)):
        stages += [up(c), down(c)]
    stages.append(finish)

    state = [{} for _ in range(tm // TOKEN_SUB)]
    for stage in stages:
        for r, st in enumerate(state):
            stage(r, st)


def _mix_ffn(x, yb, params, l, tm, casts=()):
    t = x.shape[0]
    steps = t // tm
    row = lambda width: pl.BlockSpec((tm, width), lambda i: (i, 0))
    c_in, c_out, c_shape, dests = _cast_plan(casts, steps)
    outs = pl.pallas_call(
        _with_casts(_mix_ffn_kernel, 2 + len(params), 1, dests),
        grid=(steps,),
        in_specs=[row(D_MODEL), row(ATTN_WIDTH)]
                 + [_layer_spec(p, l if stacked else None) for p, stacked in params] + c_in,
        out_specs=[row(D_MODEL)] + c_out,
        out_shape=[jax.ShapeDtypeStruct((t, D_MODEL), F32)] + c_shape,
        compiler_params=pltpu.CompilerParams(dimension_semantics=("arbitrary",),
                                             vmem_limit_bytes=VMEM_LIMIT_BYTES),
        name="mix_ffn",
    )(x, yb, *[p for p, _ in params], *[c[0] for c in casts])
    return outs[0], outs[1:]


def _rope_tables(positions):
    half = HEAD_DIM // 2
    inv_freq = 1.0 / (ROPE_THETA ** (jnp.arange(0, HEAD_DIM, 2, dtype=F32) / HEAD_DIM))
    ang = positions.astype(F32).reshape(-1, 1) * inv_freq
    cos = jnp.cos(ang)
    sin = jnp.sin(ang)
    cos_t = jnp.tile(cos, (1, LANES // half))
    sin_t = jnp.tile(jnp.concatenate([-sin, sin], axis=1), (1, HEADS_PER_TILE))
    return cos_t, sin_t


def kernel(x, positions, w_in, w_s, b_s, ln_v_g, ln_v_b, w_proj_a, w_proj_b, w_out,
           g_mix_pre, g_mix_post, g_ffn_pre, g_ffn_post, w_gate_up, w_down):
    b, s, d = x.shape
    depth = w_in.shape[0]
    t = b * s
    cos_t, sin_t = _rope_tables(positions)
    xf = x.reshape(t, d)
    gains = lambda g: g.reshape(depth, 1, -1)
    ws = w_s.reshape(depth, GM_GROUPS // 2, 2, GM_CHUNK, GM_CHUNK).transpose(0, 1, 3, 2, 4)
    ws = ws.reshape(depth, GM_GROUPS // 2, GM_CHUNK, 2 * GM_CHUNK).astype(BF16)
    bmat = jnp.repeat(b_s.transpose(0, 2, 1), GM_GROUP_DIM, axis=2)
    LATE_KEYS = ("mix", "gu", "pa", "pb", "o")
    ALL_KEYS = ("mix", "qkv", "gu", "pa", "pb", "o")

    def attn_casts(layer, late_only):
        whole = lambda stack: (stack, layer, (None,))
        in_dests = (MIX_COLS,) if late_only else (MIX_COLS, QKV_COLS)
        return [(w_in, layer, in_dests), whole(w_gate_up), whole(w_proj_a), whole(w_proj_b), whole(w_out)]

    weights = {0: {"qkv": w_in[0][:, QKV_LO:GATE_LO].astype(BF16)}}
    for l in range(depth):
        nxt = l + 1 < depth
        down_casts = ([(w_down, 0, (None,))] if l == 0 else []) + ([(w_down, l + 1, (None,))] if nxt else [])
        (q, k, va), down_out = _qkv_proj(xf, gains(g_mix_pre), weights[l]["qkv"], cos_t, sin_t, l, tm=1024,
                                         casts=down_casts)
        down_out = list(down_out)
        if l == 0:
            weights[0]["d"] = down_out.pop(0)
        casts = (attn_casts(0, True) if l == 0 else []) + (attn_casts(l + 1, False) if nxt else [])
        yb, cast_out = _moba(q.reshape(b, s, -1), k.reshape(b, s, -1), va.reshape(b, s, -1), casts)
        cast_out = list(cast_out)
        if l == 0:
            weights[0].update(zip(LATE_KEYS, cast_out[:len(LATE_KEYS)]))
            cast_out = cast_out[len(LATE_KEYS):]
        if nxt:
            weights[l + 1] = dict(zip(ALL_KEYS, cast_out))
            weights[l + 1]["d"] = down_out.pop(0)
        w = weights.pop(l)
        params = [(gains(g_mix_pre), True), (w["mix"], False), (gains(ln_v_g), True), (gains(ln_v_b), True),
                  (ws, True), (bmat, True), (w["pa"], False), (w["pb"], False), (w["o"], False),
                  (gains(g_mix_post), True), (gains(g_ffn_pre), True), (w["gu"], False), (w["d"], False),
                  (gains(g_ffn_post), True)]
        xf, _ = _mix_ffn(xf, yb.reshape(t, -1), params, l, tm=512)
    return xf.reshape(b, s, d)
```

```python
import jax
import jax.numpy as jnp
from jax import lax
from jax.experimental import pallas as pl
from jax.experimental.pallas import tpu as pltpu

D_MODEL = 1024
GM_GROUPS = 8
GM_GROUP_DIM = 64
GM_WIDTH = GM_GROUPS * GM_GROUP_DIM
GM_CHUNK = 128
ATTN_HEADS = 8
HEAD_DIM = 64
ATTN_WIDTH = ATTN_HEADS * HEAD_DIM
MOBA_BLOCK = 256
MOBA_TOPK = 3
ROPE_THETA = 10000.0
D_FF = 2816
IN_WIDTH = 2 * GM_WIDTH + 3 * ATTN_WIDTH + 2 * D_MODEL
NORM_EPS = 1e-6
NEG_INF = -1e30
LOG2_E = 1.4426950408889634

QKV_LO = 2 * GM_WIDTH
GATE_LO = QKV_LO + 3 * ATTN_WIDTH
MIX_COLS = ((0, QKV_LO), (GATE_LO, IN_WIDTH))
QKV_COLS = ((QKV_LO, GATE_LO),)

LANES = 128
HEADS_PER_TILE = LANES // HEAD_DIM
VMEM_LIMIT_BYTES = 56 * 1024 * 1024
MXU_TILE = 256

TOKEN_SUB = 256
FFN_CHUNKS = ((0, 5 * MXU_TILE), (5 * MXU_TILE, D_FF - 5 * MXU_TILE))

F32 = jnp.float32
BF16 = jnp.bfloat16


def _rms(x, g):
    return x * lax.rsqrt(jnp.mean(x * x, axis=-1, keepdims=True) + NORM_EPS) * g


def _layer_spec(param, l):
    if l is None:
        return pl.BlockSpec(param.shape, lambda *_: (0,) * param.ndim, pipeline_mode=pl.Buffered(1))
    shape = param.shape[1:]
    return pl.BlockSpec((None,) + shape, lambda *_: (l,) + (0,) * len(shape),
                        pipeline_mode=pl.Buffered(1))


def _cast_plan(items, steps):
    in_specs, out_specs, out_shapes, dests_per_item = [], [], [], []
    for w, layer, dests, rep in items:
        _, rows, cols = w.shape
        rb = rows * rep // steps
        in_specs.append(pl.BlockSpec((None, rb, cols), lambda i, layer=layer, rep=rep: (layer, i // rep, 0)))
        dests = tuple(((0, cols),) if d is None else d for d in dests)
        for d in dests:
            width = sum(hi - lo for lo, hi in d)
            out_specs.append(pl.BlockSpec((rb, width), lambda i, rep=rep: (i // rep, 0)))
            out_shapes.append(jax.ShapeDtypeStruct((rows, width), BF16))
        dests_per_item.append(dests)
    return in_specs, out_specs, out_shapes, dests_per_item


def _with_casts(body, n_in, n_out, dests_per_item):
    n_src = len(dests_per_item)
    n_dst = sum(len(d) for d in dests_per_item)

    def kernel_fn(*refs):
        ins, rest = refs[:n_in], refs[n_in:]
        srcs, rest = rest[:n_src], rest[n_src:]
        outs, rest = rest[:n_out], rest[n_out:]
        dsts, scratch = list(rest[:n_dst]), rest[n_dst:]
        for src, dests in zip(srcs, dests_per_item):
            for ranges in dests:
                dst = dsts.pop(0)
                off = 0
                for lo, hi in ranges:
                    dst[:, off:off + hi - lo] = src[:, lo:hi].astype(BF16)
                    off += hi - lo
        body(*ins, *outs, *scratch)
    return kernel_fn


def _qkv_sections(cos_ref, sin_ref, q_ref, k_ref, va_ref):
    first_half = (lax.broadcasted_iota(jnp.int32, (TOKEN_SUB, LANES), 1) % HEAD_DIM) < HEAD_DIM // 2

    def rope_out(out_ref, scale):
        def ep(z, rows):
            cos = cos_ref[rows, :]
            sin = sin_ref[rows, :]
            for c in range(ATTN_WIDTH // LANES):
                zc = z[:, c * LANES:(c + 1) * LANES]
                partner = jnp.where(first_half,
                                    pltpu.roll(zc, LANES - HEAD_DIM // 2, 1),
                                    pltpu.roll(zc, HEAD_DIM // 2, 1))
                r = zc * cos + partner * sin
                if scale != 1.0:
                    r = r * scale
                out_ref[rows, c * LANES:(c + 1) * LANES] = r.astype(BF16)
        return ep

    def cast_out(z, rows):
        va_ref[rows, :] = z.astype(BF16)

    return [
        (0, rope_out(q_ref, HEAD_DIM ** -0.5 * LOG2_E)),
        (ATTN_WIDTH, rope_out(k_ref, 1.0)),
        (2 * ATTN_WIDTH, cast_out),
    ]


def _qkv_proj_kernel(x_ref, g_ref, w_ref, cos_ref, sin_ref, q_ref, k_ref, va_ref):
    tm = x_ref.shape[0]
    sections = _qkv_sections(cos_ref, sin_ref, q_ref, k_ref, va_ref)
    items = [(r, sec) for r in range(tm // TOKEN_SUB) for sec in sections]

    h = {}
    pending = None
    for r, (lo, ep) in items:
        rows = slice(r * TOKEN_SUB, (r + 1) * TOKEN_SUB)
        if r not in h:
            h[r] = _rms(x_ref[rows, :], g_ref[...]).astype(BF16)
        z = jnp.dot(h[r], w_ref[:, lo:lo + ATTN_WIDTH], preferred_element_type=F32)
        if pending is not None:
            pending[0](pending[1], pending[2])
        pending = (ep, z, rows)
    pending[0](pending[1], pending[2])


def _qkv_proj(x, g, w_qkv, cos, sin, l, tm, casts=()):
    t = x.shape[0]
    steps = t // tm
    row = lambda width: pl.BlockSpec((tm, width), lambda i: (i, 0))
    c_in, c_out, c_shape, dests = _cast_plan(casts, steps)
    outs = pl.pallas_call(
        _with_casts(_qkv_proj_kernel, 5, 3, dests),
        grid=(steps,),
        in_specs=[row(D_MODEL), _layer_spec(g, l), _layer_spec(w_qkv, None), row(LANES), row(LANES)] + c_in,
        out_specs=[row(ATTN_WIDTH)] * 3 + c_out,
        out_shape=[jax.ShapeDtypeStruct((t, ATTN_WIDTH), BF16)] * 3 + c_shape,
        compiler_params=pltpu.CompilerParams(dimension_semantics=("arbitrary",),
                                             vmem_limit_bytes=VMEM_LIMIT_BYTES),
        name="qkv_proj",
    )(x, g, w_qkv, cos, sin, *[c[0] for c in casts])
    return outs[:3], outs[3:]


def _moba_kernel(q_ref, k_ref, v_ref, o_ref, vt_ref, kbar_ref):
    nb = k_ref.shape[1] // MOBA_BLOCK
    lane = lax.broadcasted_iota(jnp.int32, (1, LANES), 1)
    key_id = lax.broadcasted_iota(jnp.int32, (MOBA_BLOCK, MOBA_BLOCK), 0)
    qry_id = lax.broadcasted_iota(jnp.int32, (MOBA_BLOCK, MOBA_BLOCK), 1)
    causal = key_id <= qry_id
    jidx = lax.broadcasted_iota(jnp.int32, (nb, MOBA_BLOCK), 0)
    drow = lax.broadcasted_iota(jnp.int32, (LANES, 1), 0)
    nt = (((1,), (1,)), ((), ()))

    vt = v_ref[0].astype(F32).T
    for hh in range(HEADS_PER_TILE):
        vt_ref[hh] = jnp.where((drow // HEAD_DIM) == hh, vt, 1.0).astype(BF16)
    for j in range(nb):
        kb = k_ref[0, j * MOBA_BLOCK:(j + 1) * MOBA_BLOCK, :]
        kbar = jnp.mean(kb.astype(F32), axis=0, keepdims=True)
        hi = kbar.astype(BF16)
        lo = (kbar - hi.astype(F32)).astype(BF16)
        kbar_ref[j:j + 1, :] = hi.astype(F32)
        kbar_ref[nb + j:nb + j + 1, :] = lo.astype(F32)
    kbar2 = kbar_ref[...].astype(BF16)

    def scores(n, hh):
        past = n * MOBA_BLOCK
        q = q_ref[0, past:past + MOBA_BLOCK, :]
        own = (lane // HEAD_DIM) == hh
        q_own = jnp.where(own, q, jnp.zeros_like(q))
        if n > MOBA_TOPK:
            g2 = lax.dot_general(kbar2, q_own, nt, preferred_element_type=F32)
            gate = jnp.where(jidx < n, g2[:nb] + g2[nb:], -jnp.inf)
            cnt = jnp.zeros((nb, MOBA_BLOCK), jnp.int32)
            for jp in range(n):
                gj = gate[jp:jp + 1, :]
                ahead = (gj > gate) | ((gj == gate) & (jp < jidx))
                cnt = cnt + jnp.where(ahead, 1, 0)
            bias = jnp.where(cnt < MOBA_TOPK, 0.0, NEG_INF)
        else:
            bias = None
        st = lax.dot_general(k_ref[0, :past + MOBA_BLOCK, :], q_own, nt,
                             preferred_element_type=F32)
        return st, bias

    def maxes(n, st, bias):
        past = n * MOBA_BLOCK
        s_blocks = [st[j * MOBA_BLOCK:(j + 1) * MOBA_BLOCK] for j in range(n)]
        s_blocks.append(jnp.where(causal, st[past:], NEG_INF))
        m = None
        for j, sj in enumerate(s_blocks):
            mj = jnp.max(sj, axis=0, keepdims=True)
            if bias is not None and j < n:
                mj = mj + bias[j:j + 1, :]
            m = mj if m is None else jnp.maximum(m, mj)
        return s_blocks, m, bias

    def probs(n, s_blocks, m, bias):
        p_blocks = []
        for j, sj in enumerate(s_blocks):
            shift = m if (bias is None or j == n) else m - bias[j:j + 1, :]
            p_blocks.append(jnp.exp2(sj - shift).astype(BF16))
        return p_blocks[0] if n == 0 else jnp.concatenate(p_blocks, axis=0)

    def output(n, hh, pt):
        acc = jnp.dot(vt_ref[hh, :, :(n + 1) * MOBA_BLOCK], pt, preferred_element_type=F32)
        ones_row = HEAD_DIM * (1 - hh)
        return acc / acc[ones_row:ones_row + 1, :]

    items = [(n, hh) for n in reversed(range(nb)) for hh in range(HEADS_PER_TILE)]
    stage_a, stage_b, outs = {}, {}, {}
    for i in range(len(items) + 2):
        if i < len(items):
            stage_a[i] = maxes(items[i][0], *scores(*items[i]))
        if 0 <= i - 1 < len(items):
            stage_b[i - 1] = probs(items[i - 1][0], *stage_a.pop(i - 1))
        if 0 <= i - 2 < len(items):
            n, hh = items[i - 2]
            outs[hh] = output(n, hh, stage_b.pop(i - 2))
            if hh == HEADS_PER_TILE - 1:
                out_t = jnp.where(drow < HEAD_DIM, outs[0], outs[1])
                o_ref[0, n * MOBA_BLOCK:(n + 1) * MOBA_BLOCK, :] = out_t.T.astype(BF16)


def _moba(q, k, v, casts=()):
    b, s, w = q.shape
    tiles = w // LANES
    nb = s // MOBA_BLOCK
    steps = b * tiles
    spec = pl.BlockSpec((1, s, LANES), lambda bp: (bp // tiles, 0, bp % tiles))
    c_in, c_out, c_shape, dests = _cast_plan(casts, steps)
    outs = pl.pallas_call(
        _with_casts(_moba_kernel, 3, 1, dests),
        grid=(steps,),
        in_specs=[spec, spec, spec] + c_in,
        out_specs=[spec] + c_out,
        out_shape=[jax.ShapeDtypeStruct((b, s, w), BF16)] + c_shape,
        scratch_shapes=[pltpu.VMEM((HEADS_PER_TILE, LANES, s), BF16),
                        pltpu.VMEM((2 * nb, LANES), F32)],
        compiler_params=pltpu.CompilerParams(dimension_semantics=("arbitrary",),
                                             vmem_limit_bytes=VMEM_LIMIT_BYTES),
        name="moba",
    )(q, k, v, *[c[0] for c in casts])
    return outs[0], outs[1:]


def _mix_ffn_kernel(x_ref, yb_ref, gin_ref, wmix_ref, lng_ref, lnb_ref, ws_ref, bmat_ref,
                    wpa_ref, wpb_ref, wo_ref, gmix_ref, gpre_ref, wgu_ref, wd_ref, gffn_ref,
                    *rest):
    next_qkv = len(rest) > 1
    o_ref = rest[4] if next_qkv else rest[0]
    tm = x_ref.shape[0]
    lane = lax.broadcasted_iota(jnp.int32, (1, LANES), 1)
    first = lane < GM_GROUP_DIM
    zero = jnp.zeros((GM_CHUNK, LANES), BF16)
    wi = lax.broadcasted_iota(jnp.int32, (GM_CHUNK, 2 * GM_CHUNK), 0)
    wj = lax.broadcasted_iota(jnp.int32, (GM_CHUNK, 2 * GM_CHUNK), 1) % GM_CHUNK
    tril = wj <= wi
    bmat = bmat_ref[...]
    w_tril = [jnp.where(tril, ws_ref[gp], jnp.zeros_like(ws_ref[gp])) for gp in range(GM_WIDTH // LANES)]
    n_chunks = TOKEN_SUB // GM_CHUNK

    def rows_of(r):
        return slice(r * TOKEN_SUB, (r + 1) * TOKEN_SUB)

    def proj(key, lo, width):
        def stage(r, st):
            if "h0" not in st:
                st["h0"] = _rms(x_ref[rows_of(r), :], gin_ref[...]).astype(BF16)
            st[key] = jnp.dot(st["h0"], wmix_ref[:, lo:lo + width], preferred_element_type=F32)
        return stage

    def gelu_u(r, st):
        st["u"] = jax.nn.gelu(st.pop("zu"))

    def gelu_ln_v(r, st):
        v = jax.nn.gelu(st.pop("zv"))
        mu = jnp.mean(v, axis=-1, keepdims=True)
        vc = v - mu
        var = jnp.mean(vc * vc, axis=-1, keepdims=True)
        st["v"] = (vc * lax.rsqrt(var + NORM_EPS) * lng_ref[...] + lnb_ref[...]).astype(BF16)

    def gating(r, st):
        v = st.pop("v")
        u = st.pop("u")
        cols = []
        for gp in range(GM_WIDTH // LANES):
            stacked = []
            for c in range(n_chunks):
                vp = v[c * GM_CHUNK:(c + 1) * GM_CHUNK, gp * LANES:(gp + 1) * LANES]
                stacked.append(jnp.concatenate([jnp.where(first, vp, zero), jnp.where(first, zero, vp)],
                                               axis=0))
            cols.append(jnp.dot(w_tril[gp], jnp.concatenate(stacked, axis=1), preferred_element_type=F32))
        ya = []
        for c in range(n_chunks):
            mixed = jnp.concatenate([col[:, c * LANES:(c + 1) * LANES] for col in cols], axis=1) + bmat
            ya.append((u[c * GM_CHUNK:(c + 1) * GM_CHUNK] * mixed).astype(BF16))
        st["ya"] = jnp.concatenate(ya, axis=0)

    def branch(key, src, w_ref, gate_key):
        def stage(r, st):
            operand = st.pop("ya") if src == "ya" else yb_ref[rows_of(r), :]
            p = jnp.dot(operand, w_ref[...], preferred_element_type=F32)
            st[key] = jax.nn.sigmoid(st.pop(gate_key)) * p
        return stage

    def project(r, st):
        merged = (st.pop("ma") + st.pop("mb")).astype(BF16)
        st.pop("h0")
        y = jnp.dot(merged, wo_ref[...], preferred_element_type=F32)
        x1 = x_ref[rows_of(r), :] + _rms(y, gmix_ref[...])
        st["x1"] = x1
        st["h"] = _rms(x1, gpre_ref[...]).astype(BF16)

    def up(c):
        lo, width = FFN_CHUNKS[c]

        def stage(r, st):
            st["gate"] = jnp.dot(st["h"], wgu_ref[:, lo:lo + width], preferred_element_type=F32)
            st["lin"] = jnp.dot(st["h"], wgu_ref[:, D_FF + lo:D_FF + lo + width], preferred_element_type=F32)
        return stage

    def down(c):
        lo, width = FFN_CHUNKS[c]

        def stage(r, st):
            act = (jax.nn.silu(st.pop("gate")) * st.pop("lin")).astype(BF16)
            part = jnp.dot(act, wd_ref[lo:lo + width, :], preferred_element_type=F32)
            st["f"] = part if c == 0 else st["f"] + part
        return stage

    def finish(r, st):
        x2 = st.pop("x1") + _rms(st.pop("f"), gffn_ref[...])
        o_ref[rows_of(r), :] = x2
        if next_qkv:
            st["hq"] = _rms(x2, rest[0][...]).astype(BF16)

    stages = [proj("zu", 0, GM_WIDTH), proj("zv", GM_WIDTH, GM_WIDTH), gelu_u, gelu_ln_v,
              proj("zga", 2 * GM_WIDTH, D_MODEL), gating, proj("zgb", 2 * GM_WIDTH + D_MODEL, D_MODEL),
              branch("ma", "ya", wpa_ref, "zga"), branch("mb", "yb", wpb_ref, "zgb"), project]
    for c in range(len(FFN_CHUNKS)):
        stages += [up(c), down(c)]
    stages.append(finish)
    if next_qkv:
        wqkv_ref = rest[1]

        def qkv_section(lo, ep):
            def stage(r, st):
                ep(jnp.dot(st["hq"], wqkv_ref[:, lo:lo + ATTN_WIDTH], preferred_element_type=F32), rows_of(r))
            return stage
        stages += [qkv_section(lo, ep) for lo, ep in _qkv_sections(rest[2], rest[3], *rest[5:8])]

    state = [{} for _ in range(tm // TOKEN_SUB)]
    for stage in stages:
        for r, st in enumerate(state):
            stage(r, st)


def _mix_ffn(x, yb, params, l, tm, casts=(), next_qkv=None):
    t = x.shape[0]
    steps = t // tm
    row = lambda width: pl.BlockSpec((tm, width), lambda i: (i, 0))
    c_in, c_out, c_shape, dests = _cast_plan(casts, steps)
    operands = [x, yb] + [p for p, _ in params]
    in_specs = [row(D_MODEL), row(ATTN_WIDTH)] + [_layer_spec(p, l if stacked else None) for p, stacked in params]
    out_specs = [row(D_MODEL)]
    out_shape = [jax.ShapeDtypeStruct((t, D_MODEL), F32)]
    if next_qkv is not None:
        g_next, w_qkv, cos, sin = next_qkv
        operands += [g_next, w_qkv, cos, sin]
        in_specs += [_layer_spec(g_next, l + 1), _layer_spec(w_qkv, None), row(LANES), row(LANES)]
        out_specs += [row(ATTN_WIDTH)] * 3
        out_shape += [jax.ShapeDtypeStruct((t, ATTN_WIDTH), BF16)] * 3
    n_out = len(out_shape)
    outs = pl.pallas_call(
        _with_casts(_mix_ffn_kernel, len(operands), n_out, dests),
        grid=(steps,),
        in_specs=in_specs + c_in,
        out_specs=out_specs + c_out,
        out_shape=out_shape + c_shape,
        compiler_params=pltpu.CompilerParams(dimension_semantics=("arbitrary",),
                                             vmem_limit_bytes=VMEM_LIMIT_BYTES),
        name="mix_ffn",
    )(*operands, *[c[0] for c in casts])
    return outs[0], (tuple(outs[1:4]) if next_qkv is not None else None), outs[n_out:]


def _rope_tables(positions):
    half = HEAD_DIM // 2
    inv_freq = 1.0 / (ROPE_THETA ** (jnp.arange(0, HEAD_DIM, 2, dtype=F32) / HEAD_DIM))
    ang = positions.astype(F32).reshape(-1, 1) * inv_freq
    cos = jnp.cos(ang)
    sin = jnp.sin(ang)
    cos_t = jnp.tile(cos, (1, LANES // half))
    sin_t = jnp.tile(jnp.concatenate([-sin, sin], axis=1), (1, HEADS_PER_TILE))
    return cos_t, sin_t


def kernel(x, positions, w_in, w_s, b_s, ln_v_g, ln_v_b, w_proj_a, w_proj_b, w_out,
           g_mix_pre, g_mix_post, g_ffn_pre, g_ffn_post, w_gate_up, w_down):
    b, s, d = x.shape
    depth = w_in.shape[0]
    t = b * s
    cos_t, sin_t = _rope_tables(positions)
    xf = x.reshape(t, d)
    gains = lambda g: g.reshape(depth, 1, -1)
    ws = w_s.reshape(depth, GM_GROUPS // 2, 2, GM_CHUNK, GM_CHUNK).transpose(0, 1, 3, 2, 4)
    ws = ws.reshape(depth, GM_GROUPS // 2, GM_CHUNK, 2 * GM_CHUNK).astype(BF16)
    bmat = jnp.repeat(b_s.transpose(0, 2, 1), GM_GROUP_DIM, axis=2)
    LATE_KEYS = ("mix", "gu", "d", "pa", "pb", "o")
    ALL_KEYS = ("mix", "qkv", "gu", "d", "pa", "pb", "o")

    def attn_casts(layer, late_only):
        whole = lambda stack, rep=1: (stack, layer, (None,), rep)
        in_dests = (MIX_COLS,) if late_only else (MIX_COLS, QKV_COLS)
        return [(w_in, layer, in_dests, 1), whole(w_gate_up), whole(w_down, 2), whole(w_proj_a),
                whole(w_proj_b), whole(w_out)]

    weights = {0: {"qkv": w_in[0][:, QKV_LO:GATE_LO].astype(BF16)}}
    qkv, _ = _qkv_proj(xf, gains(g_mix_pre), weights[0]["qkv"], cos_t, sin_t, 0, tm=1024)
    for l in range(depth):
        nxt = l + 1 < depth
        casts = (attn_casts(0, True) if l == 0 else []) + (attn_casts(l + 1, False) if nxt else [])
        q, k, va = (a.reshape(b, s, -1) for a in qkv)
        yb, cast_out = _moba(q, k, va, casts)
        cast_out = list(cast_out)
        if l == 0:
            weights[0].update(zip(LATE_KEYS, cast_out[:len(LATE_KEYS)]))
            cast_out = cast_out[len(LATE_KEYS):]
        if nxt:
            weights[l + 1] = dict(zip(ALL_KEYS, cast_out))
        w = weights.pop(l)
        params = [(gains(g_mix_pre), True), (w["mix"], False), (gains(ln_v_g), True), (gains(ln_v_b), True),
                  (ws, True), (bmat, True), (w["pa"], False), (w["pb"], False), (w["o"], False),
                  (gains(g_mix_post), True), (gains(g_ffn_pre), True), (w["gu"], False), (w["d"], False),
                  (gains(g_ffn_post), True)]
        next_qkv = (gains(g_mix_pre), weights[l + 1]["qkv"], cos_t, sin_t) if nxt else None
        xf, qkv, _ = _mix_ffn(xf, yb.reshape(t, -1), params, l, tm=512, next_qkv=next_qkv)
    return xf.reshape(b, s, d)
```

```python
import jax
import jax.numpy as jnp
from jax import lax
from jax.experimental import pallas as pl
from jax.experimental.pallas import tpu as pltpu

D_MODEL = 1024
GM_GROUPS = 8
GM_GROUP_DIM = 64
GM_WIDTH = GM_GROUPS * GM_GROUP_DIM
GM_CHUNK = 128
ATTN_HEADS = 8
HEAD_DIM = 64
ATTN_WIDTH = ATTN_HEADS * HEAD_DIM
MOBA_BLOCK = 256
MOBA_TOPK = 3
ROPE_THETA = 10000.0
D_FF = 2816
IN_WIDTH = 2 * GM_WIDTH + 3 * ATTN_WIDTH + 2 * D_MODEL
NORM_EPS = 1e-6
NEG_INF = -1e30
LOG2_E = 1.4426950408889634

QKV_LO = 2 * GM_WIDTH
GATE_LO = QKV_LO + 3 * ATTN_WIDTH
MIX_COLS = ((0, QKV_LO), (GATE_LO, IN_WIDTH))
QKV_COLS = ((QKV_LO, GATE_LO),)

LANES = 128
HEADS_PER_TILE = LANES // HEAD_DIM
BF16_ROWS = 16
VT_ROWS = HEAD_DIM + BF16_ROWS
VMEM_LIMIT_BYTES = 56 * 1024 * 1024
MXU_TILE = 256

MIX_FFN_TILE = 512
QKV_TILE = 2048

TOKEN_SUB = 256
FFN_CHUNKS = ((0, 5 * MXU_TILE), (5 * MXU_TILE, D_FF - 5 * MXU_TILE))

F32 = jnp.float32
BF16 = jnp.bfloat16


def _rms(x, g):
    return x * lax.rsqrt(jnp.mean(x * x, axis=-1, keepdims=True) + NORM_EPS) * g


def _layer_spec(param, l):
    if l is None:
        return pl.BlockSpec(param.shape, lambda *_: (0,) * param.ndim, pipeline_mode=pl.Buffered(1))
    shape = param.shape[1:]
    return pl.BlockSpec((None,) + shape, lambda *_: (l,) + (0,) * len(shape),
                        pipeline_mode=pl.Buffered(1))


def _cast_plan(items, steps):
    in_specs, out_specs, out_shapes, dests_per_item = [], [], [], []
    for w, layer, dests, rep in items:
        _, rows, cols = w.shape
        rb = rows * rep // steps
        in_specs.append(pl.BlockSpec((None, rb, cols), lambda i, layer=layer, rep=rep: (layer, i // rep, 0)))
        dests = tuple(((0, cols),) if d is None else d for d in dests)
        for d in dests:
            width = sum(hi - lo for lo, hi in d)
            out_specs.append(pl.BlockSpec((rb, width), lambda i, rep=rep: (i // rep, 0)))
            out_shapes.append(jax.ShapeDtypeStruct((rows, width), BF16))
        dests_per_item.append(dests)
    return in_specs, out_specs, out_shapes, dests_per_item


def _with_casts(body, n_in, n_out, dests_per_item):
    n_src = len(dests_per_item)
    n_dst = sum(len(d) for d in dests_per_item)

    def kernel_fn(*refs):
        ins, rest = refs[:n_in], refs[n_in:]
        srcs, rest = rest[:n_src], rest[n_src:]
        outs, rest = rest[:n_out], rest[n_out:]
        dsts, scratch = list(rest[:n_dst]), rest[n_dst:]
        for src, dests in zip(srcs, dests_per_item):
            for ranges in dests:
                dst = dsts.pop(0)
                off = 0
                for lo, hi in ranges:
                    dst[:, off:off + hi - lo] = src[:, lo:hi].astype(BF16)
                    off += hi - lo
        body(*ins, *outs, *scratch)
    return kernel_fn


def _qkv_sections(cos_ref, sin_ref, q_ref, k_ref, va_ref):
    first_half = (lax.broadcasted_iota(jnp.int32, (TOKEN_SUB, LANES), 1) % HEAD_DIM) < HEAD_DIM // 2

    def rope_out(out_ref, scale):
        def ep(z, rows):
            cos = cos_ref[rows, :]
            sin = sin_ref[rows, :]
            for c in range(ATTN_WIDTH // LANES):
                zc = z[:, c * LANES:(c + 1) * LANES]
                partner = jnp.where(first_half,
                                    pltpu.roll(zc, LANES - HEAD_DIM // 2, 1),
                                    pltpu.roll(zc, HEAD_DIM // 2, 1))
                r = zc * cos + partner * sin
                if scale != 1.0:
                    r = r * scale
                out_ref[rows, c * LANES:(c + 1) * LANES] = r.astype(BF16)
        return ep

    def cast_out(z, rows):
        va_ref[rows, :] = z.astype(BF16)

    return [
        (0, rope_out(q_ref, HEAD_DIM ** -0.5 * LOG2_E)),
        (ATTN_WIDTH, rope_out(k_ref, 1.0)),
        (2 * ATTN_WIDTH, cast_out),
    ]


def _qkv_proj_kernel(x_ref, g_ref, w_ref, cos_ref, sin_ref, q_ref, k_ref, va_ref):
    tm = x_ref.shape[0]
    sections = _qkv_sections(cos_ref, sin_ref, q_ref, k_ref, va_ref)
    items = [(r, sec) for r in range(tm // TOKEN_SUB) for sec in sections]

    h = {}
    pending = None
    for r, (lo, ep) in items:
        rows = slice(r * TOKEN_SUB, (r + 1) * TOKEN_SUB)
        if r not in h:
            h[r] = _rms(x_ref[rows, :], g_ref[...]).astype(BF16)
        z = jnp.dot(h[r], w_ref[:, lo:lo + ATTN_WIDTH], preferred_element_type=F32)
        if pending is not None:
            pending[0](pending[1], pending[2])
        pending = (ep, z, rows)
    pending[0](pending[1], pending[2])


def _qkv_proj(x, g, w_qkv, cos, sin, l, tm, casts=()):
    t = x.shape[0]
    steps = t // tm
    row = lambda width: pl.BlockSpec((tm, width), lambda i: (i, 0))
    c_in, c_out, c_shape, dests = _cast_plan(casts, steps)
    outs = pl.pallas_call(
        _with_casts(_qkv_proj_kernel, 5, 3, dests),
        grid=(steps,),
        in_specs=[row(D_MODEL), _layer_spec(g, l), _layer_spec(w_qkv, None), row(LANES), row(LANES)] + c_in,
        out_specs=[row(ATTN_WIDTH)] * 3 + c_out,
        out_shape=[jax.ShapeDtypeStruct((t, ATTN_WIDTH), BF16)] * 3 + c_shape,
        compiler_params=pltpu.CompilerParams(dimension_semantics=("arbitrary",),
                                             vmem_limit_bytes=VMEM_LIMIT_BYTES),
        name="qkv_proj",
    )(x, g, w_qkv, cos, sin, *[c[0] for c in casts])
    return outs[:3], outs[3:]


def _moba_kernel(q_ref, k_ref, v_ref, o_ref, vt_ref, kbar_ref):
    nb = k_ref.shape[1] // MOBA_BLOCK
    lane = lax.broadcasted_iota(jnp.int32, (1, LANES), 1)
    key_id = lax.broadcasted_iota(jnp.int32, (MOBA_BLOCK, MOBA_BLOCK), 0)
    qry_id = lax.broadcasted_iota(jnp.int32, (MOBA_BLOCK, MOBA_BLOCK), 1)
    causal = key_id <= qry_id
    jidx = lax.broadcasted_iota(jnp.int32, (nb, MOBA_BLOCK), 0)
    nt =(((1,), (1,)), ((), ()))

    vt = v_ref[0].astype(F32).T.astype(BF16)
    ones = jnp.ones((VT_ROWS - HEAD_DIM, vt.shape[1]), BF16)
    for hh in range(HEADS_PER_TILE):
        vt_ref[hh] = jnp.concatenate([vt[hh * HEAD_DIM:(hh + 1) * HEAD_DIM], ones], axis=0)
    for j in range(nb):
        kb = k_ref[0, j * MOBA_BLOCK:(j + 1) * MOBA_BLOCK, :]
        kbar = jnp.mean(kb.astype(F32), axis=0, keepdims=True)
        hi = kbar.astype(BF16)
        lo = (kbar - hi.astype(F32)).astype(BF16)
        kbar_ref[j:j + 1, :] = hi.astype(F32)
        kbar_ref[nb + j:nb + j + 1, :] = lo.astype(F32)
    kbar2 = kbar_ref[...].astype(BF16)

    def scores(n, hh):
        past = n * MOBA_BLOCK
        q = q_ref[0, past:past + MOBA_BLOCK, :]
        own = (lane // HEAD_DIM) == hh
        q_own = jnp.where(own, q, jnp.zeros_like(q))
        if n > MOBA_TOPK:
            g2 = lax.dot_general(kbar2, q_own, nt, preferred_element_type=F32)
            gate = jnp.where(jidx < n, g2[:nb] + g2[nb:], -jnp.inf)
            cnt = jnp.zeros((nb, MOBA_BLOCK), jnp.int32)
            for jp in range(n):
                gj = gate[jp:jp + 1, :]
                ahead = (gj > gate) | ((gj == gate) & (jp < jidx))
                cnt = cnt + jnp.where(ahead, 1, 0)
            bias = jnp.where(cnt < MOBA_TOPK, 0.0, NEG_INF)
        else:
            bias = None
        st = lax.dot_general(k_ref[0, :past + MOBA_BLOCK, :], q_own, nt,
                             preferred_element_type=F32)
        return st, bias

    def maxes(n, st, bias):
        past = n * MOBA_BLOCK
        s_blocks = [st[j * MOBA_BLOCK:(j + 1) * MOBA_BLOCK] for j in range(n)]
        s_blocks.append(jnp.where(causal, st[past:], NEG_INF))
        m = None
        for j, sj in enumerate(s_blocks):
            mj = jnp.max(sj, axis=0, keepdims=True)
            if bias is not None and j < n:
                mj = mj + bias[j:j + 1, :]
            m = mj if m is None else jnp.maximum(m, mj)
        return s_blocks, m, bias

    def probs(n, s_blocks, m, bias):
        p_blocks = []
        for j, sj in enumerate(s_blocks):
            shift = m if (bias is None or j == n) else m - bias[j:j + 1, :]
            p_blocks.append(jnp.exp2(sj - shift).astype(BF16))
        return p_blocks[0] if n == 0 else jnp.concatenate(p_blocks, axis=0)

    def output(n, hh, pt):
        acc = jnp.dot(vt_ref[hh, :, :(n + 1) * MOBA_BLOCK], pt, preferred_element_type=F32)
        return acc[:HEAD_DIM] / acc[HEAD_DIM:HEAD_DIM + 1, :]

    items = [(n, hh) for n in reversed(range(nb)) for hh in range(HEADS_PER_TILE)]
    stage_a, stage_b, outs = {}, {}, {}
    for i in range(len(items) + 2):
        if i < len(items):
            stage_a[i] = maxes(items[i][0], *scores(*items[i]))
        if 0 <= i - 1 < len(items):
            stage_b[i - 1] = probs(items[i - 1][0], *stage_a.pop(i - 1))
        if 0 <= i - 2 < len(items):
            n, hh = items[i - 2]
            outs[hh] = output(n, hh, stage_b.pop(i - 2))
            if hh == HEADS_PER_TILE - 1:
                out_t = jnp.concatenate([outs[0], outs[1]], axis=0)
                o_ref[0, n * MOBA_BLOCK:(n + 1) * MOBA_BLOCK, :] = out_t.T.astype(BF16)


def _moba(q, k, v, casts=()):
    b, s, w = q.shape
    tiles = w // LANES
    nb = s // MOBA_BLOCK
    steps = b * tiles
    spec = pl.BlockSpec((1, s, LANES), lambda bp: (bp // tiles, 0, bp % tiles))
    c_in, c_out, c_shape, dests = _cast_plan(casts, steps)
    outs = pl.pallas_call(
        _with_casts(_moba_kernel, 3, 1, dests),
        grid=(steps,),
        in_specs=[spec, spec, spec] + c_in,
        out_specs=[spec] + c_out,
        out_shape=[jax.ShapeDtypeStruct((b, s, w), BF16)] + c_shape,
        scratch_shapes=[pltpu.VMEM((HEADS_PER_TILE, VT_ROWS, s), BF16),
                        pltpu.VMEM((2 * nb, LANES), F32)],
        compiler_params=pltpu.CompilerParams(dimension_semantics=("arbitrary",),
                                             vmem_limit_bytes=VMEM_LIMIT_BYTES),
        name="moba",
    )(q, k, v, *[c[0] for c in casts])
    return outs[0], outs[1:]


def _mix_ffn_kernel(x_ref, yb_ref, gin_ref, wmix_ref, lng_ref, lnb_ref, ws_ref, bmat_ref,
                    wpa_ref, wpb_ref, wo_ref, gmix_ref, gpre_ref, wgu_ref, wd_ref, gffn_ref,
                    *rest):
    next_qkv = len(rest) > 1
    o_ref = rest[4] if next_qkv else rest[0]
    tm = x_ref.shape[0]
    lane = lax.broadcasted_iota(jnp.int32, (1, LANES), 1)
    first = lane < GM_GROUP_DIM
    zero = jnp.zeros((GM_CHUNK, LANES), BF16)
    wi = lax.broadcasted_iota(jnp.int32, (GM_CHUNK, 2 * GM_CHUNK), 0)
    wj = lax.broadcasted_iota(jnp.int32, (GM_CHUNK, 2 * GM_CHUNK), 1) % GM_CHUNK
    tril = wj <= wi
    bmat = bmat_ref[...]
    w_tril = [jnp.where(tril, ws_ref[gp], jnp.zeros_like(ws_ref[gp])) for gp in range(GM_WIDTH // LANES)]
    n_chunks = TOKEN_SUB // GM_CHUNK

    def rows_of(r):
        return slice(r * TOKEN_SUB, (r + 1) * TOKEN_SUB)

    def proj(key, lo, width):
        def stage(r, st):
            if "h0" not in st:
                st["h0"] = _rms(x_ref[rows_of(r), :], gin_ref[...]).astype(BF16)
            st[key] = jnp.dot(st["h0"], wmix_ref[:, lo:lo + width], preferred_element_type=F32)
        return stage

    def gelu_u(r, st):
        st["u"] = jax.nn.gelu(st.pop("zu"))

    def gelu_ln_v(r, st):
        v = jax.nn.gelu(st.pop("zv"))
        mu = jnp.mean(v, axis=-1, keepdims=True)
        vc = v - mu
        var = jnp.mean(vc * vc, axis=-1, keepdims=True)
        st["v"] = (vc * lax.rsqrt(var + NORM_EPS) * lng_ref[...] + lnb_ref[...]).astype(BF16)

    def gating(r, st):
        v = st.pop("v")
        u = st.pop("u")
        cols = []
        for gp in range(GM_WIDTH // LANES):
            stacked = []
            for c in range(n_chunks):
                vp = v[c * GM_CHUNK:(c + 1) * GM_CHUNK, gp * LANES:(gp + 1) * LANES]
                stacked.append(jnp.concatenate([jnp.where(first, vp, zero), jnp.where(first, zero, vp)],
                                               axis=0))
            cols.append(jnp.dot(w_tril[gp], jnp.concatenate(stacked, axis=1), preferred_element_type=F32))
        ya = []
        for c in range(n_chunks):
            mixed = jnp.concatenate([col[:, c * LANES:(c + 1) * LANES] for col in cols], axis=1) + bmat
            ya.append((u[c * GM_CHUNK:(c + 1) * GM_CHUNK] * mixed).astype(BF16))
        st["ya"] = jnp.concatenate(ya, axis=0)

    def branch(key, src, w_ref, gate_key):
        def stage(r, st):
            operand = st.pop("ya") if src == "ya" else yb_ref[rows_of(r), :]
            p = jnp.dot(operand, w_ref[...], preferred_element_type=F32)
            st[key] = jax.nn.sigmoid(st.pop(gate_key)) * p
        return stage

    def project(r, st):
        merged = (st.pop("ma") + st.pop("mb")).astype(BF16)
        st.pop("h0")
        y = jnp.dot(merged, wo_ref[...], preferred_element_type=F32)
        x1 = x_ref[rows_of(r), :] + _rms(y, gmix_ref[...])
        st["x1"] = x1
        st["h"] = _rms(x1, gpre_ref[...]).astype(BF16)

    def up(c):
        lo, width = FFN_CHUNKS[c]

        def stage(r, st):
            st["gate"] = jnp.dot(st["h"], wgu_ref[:, lo:lo + width], preferred_element_type=F32)
            st["lin"] = jnp.dot(st["h"], wgu_ref[:, D_FF + lo:D_FF + lo + width], preferred_element_type=F32)
        return stage

    def down(c):
        lo, width = FFN_CHUNKS[c]

        def stage(r, st):
            act = (jax.nn.silu(st.pop("gate")) * st.pop("lin")).astype(BF16)
            part = jnp.dot(act, wd_ref[lo:lo + width, :], preferred_element_type=F32)
            st["f"] = part if c == 0 else st["f"] + part
        return stage

    def finish(r, st):
        x2 = st.pop("x1") + _rms(st.pop("f"), gffn_ref[...])
        o_ref[rows_of(r), :] = x2
        if next_qkv:
            st["hq"] = _rms(x2, rest[0][...]).astype(BF16)

    stages = [proj("zu", 0, GM_WIDTH), proj("zv", GM_WIDTH, GM_WIDTH), gelu_u, gelu_ln_v,
              proj("zga", 2 * GM_WIDTH, D_MODEL), gating, proj("zgb", 2 * GM_WIDTH + D_MODEL, D_MODEL),
              branch("ma", "ya", wpa_ref, "zga"), branch("mb", "yb", wpb_ref, "zgb"), project]
    for c in range(len(FFN_CHUNKS)):
        stages += [up(c), down(c)]
    stages.append(finish)
    if next_qkv:
        wqkv_ref = rest[1]

        def qkv_section(lo, ep):
            def stage(r, st):
                ep(jnp.dot(st["hq"], wqkv_ref[:, lo:lo + ATTN_WIDTH], preferred_element_type=F32), rows_of(r))
            return stage
        stages += [qkv_section(lo, ep) for lo, ep in _qkv_sections(rest[2], rest[3], *rest[5:8])]

    state = [{} for _ in range(tm // TOKEN_SUB)]
    for stage in stages:
        for r, st in enumerate(state):
            stage(r, st)


def _mix_ffn(x, yb, params, l, tm, casts=(), next_qkv=None):
    t = x.shape[0]
    steps = t // tm
    row = lambda width: pl.BlockSpec((tm, width), lambda i: (i, 0))
    c_in, c_out, c_shape, dests = _cast_plan(casts, steps)
    operands = [x, yb] + [p for p, _ in params]
    in_specs = [row(D_MODEL), row(ATTN_WIDTH)] + [_layer_spec(p, l if stacked else None) for p, stacked in params]
    out_specs = [row(D_MODEL)]
    out_shape = [jax.ShapeDtypeStruct((t, D_MODEL), F32)]
    if next_qkv is not None:
        g_next, w_qkv, cos, sin = next_qkv
        operands += [g_next, w_qkv, cos, sin]
        in_specs += [_layer_spec(g_next, l + 1), _layer_spec(w_qkv, None), row(LANES), row(LANES)]
        out_specs += [row(ATTN_WIDTH)] * 3
        out_shape += [jax.ShapeDtypeStruct((t, ATTN_WIDTH), BF16)] * 3
    n_out = len(out_shape)
    outs = pl.pallas_call(
        _with_casts(_mix_ffn_kernel, len(operands), n_out, dests),
        grid=(steps,),
        in_specs=in_specs + c_in,
        out_specs=out_specs + c_out,
        out_shape=out_shape + c_shape,
        compiler_params=pltpu.CompilerParams(dimension_semantics=("arbitrary",),
                                             vmem_limit_bytes=VMEM_LIMIT_BYTES),
        name="mix_ffn",
    )(*operands, *[c[0] for c in casts])
    return outs[0], (tuple(outs[1:4]) if next_qkv is not None else None), outs[n_out:]


def _rope_tables(positions):
    half = HEAD_DIM // 2
    inv_freq = 1.0 / (ROPE_THETA ** (jnp.arange(0, HEAD_DIM, 2, dtype=F32) / HEAD_DIM))
    ang = positions.astype(F32).reshape(-1, 1) * inv_freq
    cos = jnp.cos(ang)
    sin = jnp.sin(ang)
    cos_t = jnp.tile(cos, (1, LANES // half))
    sin_t = jnp.tile(jnp.concatenate([-sin, sin], axis=1), (1, HEADS_PER_TILE))
    return cos_t, sin_t


def kernel(x, positions, w_in, w_s, b_s, ln_v_g, ln_v_b, w_proj_a, w_proj_b, w_out,
           g_mix_pre, g_mix_post, g_ffn_pre, g_ffn_post, w_gate_up, w_down):
    b, s, d = x.shape
    depth = w_in.shape[0]
    t = b * s
    cos_t, sin_t = _rope_tables(positions)
    xf = x.reshape(t, d)
    gains = lambda g: g.reshape(depth, 1, -1)
    ws = w_s.reshape(depth, GM_GROUPS // 2, 2, GM_CHUNK, GM_CHUNK).transpose(0, 1, 3, 2, 4)
    ws = ws.reshape(depth, GM_GROUPS // 2, GM_CHUNK, 2 * GM_CHUNK).astype(BF16)
    bmat = jnp.repeat(b_s.transpose(0, 2, 1), GM_GROUP_DIM, axis=2)
    LATE_KEYS = ("mix", "gu", "d", "pa", "pb", "o")
    ALL_KEYS = ("mix", "qkv", "gu", "d", "pa", "pb", "o")

    def attn_casts(layer, late_only):
        whole = lambda stack, rep=1: (stack, layer, (None,), rep)
        in_dests = (MIX_COLS,) if late_only else (MIX_COLS, QKV_COLS)
        return [(w_in, layer, in_dests, 1), whole(w_gate_up), whole(w_down, 2), whole(w_proj_a),
                whole(w_proj_b), whole(w_out)]

    weights = {0: {"qkv": w_in[0][:, QKV_LO:GATE_LO].astype(BF16)}}
    qkv, _ = _qkv_proj(xf, gains(g_mix_pre), weights[0]["qkv"], cos_t, sin_t, 0, tm=QKV_TILE)
    for l in range(depth):
        nxt = l + 1 < depth
        casts = (attn_casts(0, True) if l == 0 else []) + (attn_casts(l + 1, False) if nxt else [])
        q, k, va = (a.reshape(b, s, -1) for a in qkv)
        yb, cast_out = _moba(q, k, va, casts)
        cast_out = list(cast_out)
        if l == 0:
            weights[0].update(zip(LATE_KEYS, cast_out[:len(LATE_KEYS)]))
            cast_out = cast_out[len(LATE_KEYS):]
        if nxt:
            weights[l + 1] = dict(zip(ALL_KEYS, cast_out))
        w = weights.pop(l)
        params = [(gains(g_mix_pre), True), (w["mix"], False), (gains(ln_v_g), True), (gains(ln_v_b), True),
                  (ws, True), (bmat, True), (w["pa"], False), (w["pb"], False), (w["o"], False),
                  (gains(g_mix_post), True), (gains(g_ffn_pre), True), (w["gu"], False), (w["d"], False),
                  (gains(g_ffn_post), True)]
        next_qkv = (gains(g_mix_pre), weights[l + 1]["qkv"], cos_t, sin_t) if nxt else None
        xf, qkv, _ = _mix_ffn(xf, yb.reshape(t, -1), params, l, tm=MIX_FFN_TILE, next_qkv=next_qkv)
    return xf.reshape(b, s, d)
```

```python
import jax
import jax.numpy as jnp
from jax import lax
from jax.experimental import pallas as pl
from jax.experimental.pallas import tpu as pltpu

D_MODEL = 1024
GM_GROUPS = 8
GM_GROUP_DIM = 64
GM_WIDTH = GM_GROUPS * GM_GROUP_DIM
GM_CHUNK = 128
ATTN_HEADS = 8
HEAD_DIM = 64
ATTN_WIDTH = ATTN_HEADS * HEAD_DIM
MOBA_BLOCK = 256
MOBA_TOPK = 3
ROPE_THETA = 10000.0
D_FF = 2816
IN_WIDTH = 2 * GM_WIDTH + 3 * ATTN_WIDTH + 2 * D_MODEL
NORM_EPS = 1e-6
NEG_INF = -1e30
LOG2_E = 1.4426950408889634

QKV_LO = 2 * GM_WIDTH
GATE_LO = QKV_LO + 3 * ATTN_WIDTH
MIX_COLS = ((0, QKV_LO), (GATE_LO, IN_WIDTH))
QKV_COLS = ((QKV_LO, GATE_LO),)

LANES = 128
HEADS_PER_TILE = LANES // HEAD_DIM
VMEM_LIMIT_BYTES = 56 * 1024 * 1024
MXU_TILE = 256

TOKEN_SUB = 256
FFN_CHUNKS = ((0, 5 * MXU_TILE), (5 * MXU_TILE, D_FF - 5 * MXU_TILE))

F32 = jnp.float32
BF16 = jnp.bfloat16


def _rms(x, g):
    return x * lax.rsqrt(jnp.mean(x * x, axis=-1, keepdims=True) + NORM_EPS) * g


def _layer_spec(param, l):
    if l is None:
        return pl.BlockSpec(param.shape, lambda *_: (0,) * param.ndim, pipeline_mode=pl.Buffered(1))
    shape = param.shape[1:]
    return pl.BlockSpec((None,) + shape, lambda *_: (l,) + (0,) * len(shape),
                        pipeline_mode=pl.Buffered(1))


def _cast_plan(items, steps):
    in_specs, out_specs, out_shapes, dests_per_item = [], [], [], []
    for w, layer, dests, rep in items:
        _, rows, cols = w.shape
        rb = rows * rep // steps
        in_specs.append(pl.BlockSpec((None, rb, cols), lambda i, layer=layer, rep=rep: (layer, i // rep, 0)))
        dests = tuple(((0, cols),) if d is None else d for d in dests)
        for d in dests:
            width = sum(hi - lo for lo, hi in d)
            out_specs.append(pl.BlockSpec((rb, width), lambda i, rep=rep: (i // rep, 0)))
            out_shapes.append(jax.ShapeDtypeStruct((rows, width), BF16))
        dests_per_item.append(dests)
    return in_specs, out_specs, out_shapes, dests_per_item


def _with_casts(body, n_in, n_out, dests_per_item):
    n_src = len(dests_per_item)
    n_dst = sum(len(d) for d in dests_per_item)

    def kernel_fn(*refs):
        ins, rest = refs[:n_in], refs[n_in:]
        srcs, rest = rest[:n_src], rest[n_src:]
        outs, rest = rest[:n_out], rest[n_out:]
        dsts, scratch = list(rest[:n_dst]), rest[n_dst:]
        for src, dests in zip(srcs, dests_per_item):
            for ranges in dests:
                dst = dsts.pop(0)
                off = 0
                for lo, hi in ranges:
                    dst[:, off:off + hi - lo] = src[:, lo:hi].astype(BF16)
                    off += hi - lo
        body(*ins, *outs, *scratch)
    return kernel_fn


def _qkv_sections(cos_ref, sin_ref, q_ref, k_ref, va_ref):
    first_half = (lax.broadcasted_iota(jnp.int32, (TOKEN_SUB, LANES), 1) % HEAD_DIM) < HEAD_DIM // 2

    def rope_out(out_ref, scale):
        def ep(z, rows):
            cos = cos_ref[rows, :]
            sin = sin_ref[rows, :]
            for c in range(ATTN_WIDTH // LANES):
                zc = z[:, c * LANES:(c + 1) * LANES]
                partner = jnp.where(first_half,
                                    pltpu.roll(zc, LANES - HEAD_DIM // 2, 1),
                                    pltpu.roll(zc, HEAD_DIM // 2, 1))
                r = zc * cos + partner * sin
                if scale != 1.0:
                    r = r * scale
                out_ref[rows, c * LANES:(c + 1) * LANES] = r.astype(BF16)
        return ep

    def cast_out(z, rows):
        va_ref[rows, :] = z.astype(BF16)

    return [
        (0, rope_out(q_ref, HEAD_DIM ** -0.5 * LOG2_E)),
        (ATTN_WIDTH, rope_out(k_ref, 1.0)),
        (2 * ATTN_WIDTH, cast_out),
    ]


def _qkv_proj_kernel(x_ref, g_ref, w_ref, cos_ref, sin_ref, q_ref, k_ref, va_ref):
    tm = x_ref.shape[0]
    sections = _qkv_sections(cos_ref, sin_ref, q_ref, k_ref, va_ref)
    items = [(r, sec) for r in range(tm // TOKEN_SUB) for sec in sections]

    h = {}
    pending = None
    for r, (lo, ep) in items:
        rows = slice(r * TOKEN_SUB, (r + 1) * TOKEN_SUB)
        if r not in h:
            h[r] = _rms(x_ref[rows, :], g_ref[...]).astype(BF16)
        z = jnp.dot(h[r], w_ref[:, lo:lo + ATTN_WIDTH], preferred_element_type=F32)
        if pending is not None:
            pending[0](pending[1], pending[2])
        pending = (ep, z, rows)
    pending[0](pending[1], pending[2])


def _qkv_proj(x, g, w_qkv, cos, sin, l, tm, casts=()):
    t = x.shape[0]
    steps = t // tm
    row = lambda width: pl.BlockSpec((tm, width), lambda i: (i, 0))
    c_in, c_out, c_shape, dests = _cast_plan(casts, steps)
    outs = pl.pallas_call(
        _with_casts(_qkv_proj_kernel, 5, 3, dests),
        grid=(steps,),
        in_specs=[row(D_MODEL), _layer_spec(g, l), _layer_spec(w_qkv, None), row(LANES), row(LANES)] + c_in,
        out_specs=[row(ATTN_WIDTH)] * 3 + c_out,
        out_shape=[jax.ShapeDtypeStruct((t, ATTN_WIDTH), BF16)] * 3 + c_shape,
        compiler_params=pltpu.CompilerParams(dimension_semantics=("arbitrary",),
                                             vmem_limit_bytes=VMEM_LIMIT_BYTES),
        name="qkv_proj",
    )(x, g, w_qkv, cos, sin, *[c[0] for c in casts])
    return outs[:3], outs[3:]


def _moba_kernel(q_ref, k_ref, v_ref, o_ref, vt_ref, kbar_ref, s0_ref, s1_ref, p0_ref, p1_ref):
    nb = k_ref.shape[1] // MOBA_BLOCK
    lane = lax.broadcasted_iota(jnp.int32, (1, LANES), 1)
    key_id = lax.broadcasted_iota(jnp.int32, (MOBA_BLOCK, MOBA_BLOCK), 0)
    qry_id = lax.broadcasted_iota(jnp.int32, (MOBA_BLOCK, MOBA_BLOCK), 1)
    causal = key_id <= qry_id
    jidx = lax.broadcasted_iota(jnp.int32, (nb, MOBA_BLOCK), 0)
    drow = lax.broadcasted_iota(jnp.int32, (LANES, 1), 0)
    nt = (((1,), (1,)), ((), ()))

    vt = v_ref[0].astype(F32).T
    for hh in range(HEADS_PER_TILE):
        vt_ref[hh] = jnp.where((drow // HEAD_DIM) == hh, vt, 1.0).astype(BF16)
    for j in range(nb):
        kb = k_ref[0, j * MOBA_BLOCK:(j + 1) * MOBA_BLOCK, :]
        kbar = jnp.mean(kb.astype(F32), axis=0, keepdims=True)
        hi = kbar.astype(BF16)
        lo = (kbar - hi.astype(F32)).astype(BF16)
        kbar_ref[j:j + 1, :] = hi.astype(F32)
        kbar_ref[nb + j:nb + j + 1, :] = lo.astype(F32)
    kbar2 = kbar_ref[...].astype(BF16)

    def scores(n, hh):
        past = n * MOBA_BLOCK
        q = q_ref[0, past:past + MOBA_BLOCK, :]
        own = (lane // HEAD_DIM) == hh
        q_own = jnp.where(own, q, jnp.zeros_like(q))
        if n > MOBA_TOPK:
            g2 = lax.dot_general(kbar2, q_own, nt, preferred_element_type=F32)
            gate = jnp.where(jidx < n, g2[:nb] + g2[nb:], -jnp.inf)
            cnt = jnp.zeros((nb, MOBA_BLOCK), jnp.int32)
            for jp in range(n):
                gj = gate[jp:jp + 1, :]
                ahead = (gj > gate) | ((gj == gate) & (jp < jidx))
                cnt = cnt + jnp.where(ahead, 1, 0)
            bias = jnp.where(cnt < MOBA_TOPK, 0.0, NEG_INF)
        else:
            bias = None
        return {"n": n, "hh": hh, "q": q_own, "bias": bias, "m": None}

    def block(j):
        return slice(j * MOBA_BLOCK, (j + 1) * MOBA_BLOCK)

    s_refs, p_refs = (s0_ref, s1_ref), (p0_ref, p1_ref)
    dyn0 = lax.shift_right_logical(pl.program_id(0), 30)

    def score_block(it, slot, j):
        n, bias = it["n"], it["bias"]
        sj = lax.dot_general(k_ref[0, block(j), :], it["q"], nt, preferred_element_type=F32)
        if j == n:
            sj = jnp.where(causal, sj, NEG_INF)
        s_refs[slot][dyn0, block(j), :] = sj
        mj = jnp.max(sj, axis=0, keepdims=True)
        if bias is not None and j < n:
            mj = mj + bias[j:j + 1, :]
        it["m"] = mj if it["m"] is None else jnp.maximum(it["m"], mj)

    def prob_block(it, slot, j):
        n, bias, m = it["n"], it["bias"], it["m"]
        shift = m if (bias is None or j == n) else m - bias[j:j + 1, :]
        p_refs[slot][dyn0, block(j), :] = jnp.exp2(s_refs[slot][dyn0, block(j), :] - shift).astype(BF16)

    def output(it, slot):
        keys = (it["n"] + 1) * MOBA_BLOCK
        acc = jnp.dot(vt_ref[it["hh"], :, :keys], p_refs[slot][dyn0, :keys, :], preferred_element_type=F32)
        ones_row = HEAD_DIM * (1 - it["hh"])
        return acc / acc[ones_row:ones_row + 1, :]

    order = [(n, hh) for n in reversed(range(nb)) for hh in range(HEADS_PER_TILE)]
    items, outs = {}, {}
    for i in range(len(order) + 2):
        if i < len(order):
            items[i] = scores(*order[i])
        for j in range(nb):
            if i in items and j <= items[i]["n"]:
                score_block(items[i], i % 2, j)
            if i - 1 in items and j <= items[i - 1]["n"]:
                prob_block(items[i - 1], (i - 1) % 2, j)
        if i - 2 in items:
            it = items.pop(i - 2)
            outs[it["hh"]] = output(it, i % 2)
            if it["hh"] == HEADS_PER_TILE - 1:
                out_t = jnp.where(drow < HEAD_DIM, outs[0], outs[1])
                o_ref[0, block(it["n"]), :] = out_t.T.astype(BF16)


def _moba(q, k, v, casts=()):
    b, s, w = q.shape
    tiles = w // LANES
    nb = s // MOBA_BLOCK
    steps = b * tiles
    spec = pl.BlockSpec((1, s, LANES), lambda bp: (bp // tiles, 0, bp % tiles))
    c_in, c_out, c_shape, dests = _cast_plan(casts, steps)
    outs = pl.pallas_call(
        _with_casts(_moba_kernel, 3, 1, dests),
        grid=(steps,),
        in_specs=[spec, spec, spec] + c_in,
        out_specs=[spec] + c_out,
        out_shape=[jax.ShapeDtypeStruct((b, s, w), BF16)] + c_shape,
        scratch_shapes=[pltpu.VMEM((HEADS_PER_TILE, LANES, s), BF16),
                        pltpu.VMEM((2 * nb, LANES), F32),
                        pltpu.VMEM((1, s, MOBA_BLOCK), F32), pltpu.VMEM((1, s, MOBA_BLOCK), F32),
                        pltpu.VMEM((1, s, MOBA_BLOCK), BF16), pltpu.VMEM((1, s, MOBA_BLOCK), BF16)],
        compiler_params=pltpu.CompilerParams(dimension_semantics=("arbitrary",),
                                             vmem_limit_bytes=VMEM_LIMIT_BYTES),
        name="moba",
    )(q, k, v, *[c[0] for c in casts])
    return outs[0], outs[1:]


def _mix_ffn_kernel(x_ref, yb_ref, gin_ref, wmix_ref, lng_ref, lnb_ref, ws_ref, bmat_ref,
                    wpa_ref, wpb_ref, wo_ref, gmix_ref, gpre_ref, wgu_ref, wd_ref, gffn_ref,
                    *rest):
    next_qkv = len(rest) > 1
    o_ref = rest[4] if next_qkv else rest[0]
    tm = x_ref.shape[0]
    lane = lax.broadcasted_iota(jnp.int32, (1, LANES), 1)
    first = lane < GM_GROUP_DIM
    zero = jnp.zeros((GM_CHUNK, LANES), BF16)
    wi = lax.broadcasted_iota(jnp.int32, (GM_CHUNK, 2 * GM_CHUNK), 0)
    wj = lax.broadcasted_iota(jnp.int32, (GM_CHUNK, 2 * GM_CHUNK), 1) % GM_CHUNK
    tril = wj <= wi
    bmat = bmat_ref[...]
    w_tril = [jnp.where(tril, ws_ref[gp], jnp.zeros_like(ws_ref[gp])) for gp in range(GM_WIDTH // LANES)]
    n_chunks = TOKEN_SUB // GM_CHUNK

    def rows_of(r):
        return slice(r * TOKEN_SUB, (r + 1) * TOKEN_SUB)

    def proj(key, lo, width):
        def stage(r, st):
            if "h0" not in st:
                st["h0"] = _rms(x_ref[rows_of(r), :], gin_ref[...]).astype(BF16)
            st[key] = jnp.dot(st["h0"], wmix_ref[:, lo:lo + width], preferred_element_type=F32)
        return stage

    def gelu_u(r, st):
        st["u"] = jax.nn.gelu(st.pop("zu"))

    def gelu_ln_v(r, st):
        v = jax.nn.gelu(st.pop("zv"))
        mu = jnp.mean(v, axis=-1, keepdims=True)
        vc = v - mu
        var = jnp.mean(vc * vc, axis=-1, keepdims=True)
        st["v"] = (vc * lax.rsqrt(var + NORM_EPS) * lng_ref[...] + lnb_ref[...]).astype(BF16)

    def gating(r, st):
        v = st.pop("v")
        u = st.pop("u")
        cols = []
        for gp in range(GM_WIDTH // LANES):
            stacked = []
            for c in range(n_chunks):
                vp = v[c * GM_CHUNK:(c + 1) * GM_CHUNK, gp * LANES:(gp + 1) * LANES]
                stacked.append(jnp.concatenate([jnp.where(first, vp, zero), jnp.where(first, zero, vp)],
                                               axis=0))
            cols.append(jnp.dot(w_tril[gp], jnp.concatenate(stacked, axis=1), preferred_element_type=F32))
        ya = []
        for c in range(n_chunks):
            mixed = jnp.concatenate([col[:, c * LANES:(c + 1) * LANES] for col in cols], axis=1) + bmat
            ya.append((u[c * GM_CHUNK:(c + 1) * GM_CHUNK] * mixed).astype(BF16))
        st["ya"] = jnp.concatenate(ya, axis=0)

    def branch(key, src, w_ref, gate_key):
        def stage(r, st):
            operand = st.pop("ya") if src == "ya" else yb_ref[rows_of(r), :]
            p = jnp.dot(operand, w_ref[...], preferred_element_type=F32)
            st[key] = jax.nn.sigmoid(st.pop(gate_key)) * p
        return stage

    def project(r, st):
        merged = (st.pop("ma") + st.pop("mb")).astype(BF16)
        st.pop("h0")
        y = jnp.dot(merged, wo_ref[...], preferred_element_type=F32)
        x1 = x_ref[rows_of(r), :] + _rms(y, gmix_ref[...])
        st["x1"] = x1
        st["h"] = _rms(x1, gpre_ref[...]).astype(BF16)

    def up(c):
        lo, width = FFN_CHUNKS[c]

        def stage(r, st):
            st["gate"] = jnp.dot(st["h"], wgu_ref[:, lo:lo + width], preferred_element_type=F32)
            st["lin"] = jnp.dot(st["h"], wgu_ref[:, D_FF + lo:D_FF + lo + width], preferred_element_type=F32)
        return stage

    def down(c):
        lo, width = FFN_CHUNKS[c]

        def stage(r, st):
            act = (jax.nn.silu(st.pop("gate")) * st.pop("lin")).astype(BF16)
            part = jnp.dot(act, wd_ref[lo:lo + width, :], preferred_element_type=F32)
            st["f"] = part if c == 0 else st["f"] + part
        return stage

    def finish(r, st):
        x2 = st.pop("x1") + _rms(st.pop("f"), gffn_ref[...])
        o_ref[rows_of(r), :] = x2
        if next_qkv:
            st["hq"] = _rms(x2, rest[0][...]).astype(BF16)

    stages = [proj("zu", 0, GM_WIDTH), proj("zv", GM_WIDTH, GM_WIDTH), gelu_u, gelu_ln_v,
              proj("zga", 2 * GM_WIDTH, D_MODEL), gating, proj("zgb", 2 * GM_WIDTH + D_MODEL, D_MODEL),
              branch("ma", "ya", wpa_ref, "zga"), branch("mb", "yb", wpb_ref, "zgb"), project]
    for c in range(len(FFN_CHUNKS)):
        stages += [up(c), down(c)]
    stages.append(finish)
    if next_qkv:
        wqkv_ref = rest[1]

        def qkv_section(lo, ep):
            def stage(r, st):
                ep(jnp.dot(st["hq"], wqkv_ref[:, lo:lo + ATTN_WIDTH], preferred_element_type=F32), rows_of(r))
            return stage
        stages += [qkv_section(lo, ep) for lo, ep in _qkv_sections(rest[2], rest[3], *rest[5:8])]

    state = [{} for _ in range(tm // TOKEN_SUB)]
    for stage in stages:
        for r, st in enumerate(state):
            stage(r, st)


def _mix_ffn(x, yb, params, l, tm, casts=(), next_qkv=None):
    t = x.shape[0]
    steps = t // tm
    row = lambda width: pl.BlockSpec((tm, width), lambda i: (i, 0))
    c_in, c_out, c_shape, dests = _cast_plan(casts, steps)
    operands = [x, yb] + [p for p, _ in params]
    in_specs = [row(D_MODEL), row(ATTN_WIDTH)] + [_layer_spec(p, l if stacked else None) for p, stacked in params]
    out_specs = [row(D_MODEL)]
    out_shape = [jax.ShapeDtypeStruct((t, D_MODEL), F32)]
    if next_qkv is not None:
        g_next, w_qkv, cos, sin = next_qkv
        operands += [g_next, w_qkv, cos, sin]
        in_specs += [_layer_spec(g_next, l + 1), _layer_spec(w_qkv, None), row(LANES), row(LANES)]
        out_specs += [row(ATTN_WIDTH)] * 3
        out_shape += [jax.ShapeDtypeStruct((t, ATTN_WIDTH), BF16)] * 3
    n_out = len(out_shape)
    outs = pl.pallas_call(
        _with_casts(_mix_ffn_kernel, len(operands), n_out, dests),
        grid=(steps,),
        in_specs=in_specs + c_in,
        out_specs=out_specs + c_out,
        out_shape=out_shape + c_shape,
        compiler_params=pltpu.CompilerParams(dimension_semantics=("arbitrary",),
                                             vmem_limit_bytes=VMEM_LIMIT_BYTES),
        name="mix_ffn",
    )(*operands, *[c[0] for c in casts])
    return outs[0], (tuple(outs[1:4]) if next_qkv is not None else None), outs[n_out:]


def _rope_tables(positions):
    half = HEAD_DIM // 2
    inv_freq = 1.0 / (ROPE_THETA ** (jnp.arange(0, HEAD_DIM, 2, dtype=F32) / HEAD_DIM))
    ang = positions.astype(F32).reshape(-1, 1) * inv_freq
    cos = jnp.cos(ang)
    sin = jnp.sin(ang)
    cos_t = jnp.tile(cos, (1, LANES // half))
    sin_t = jnp.tile(jnp.concatenate([-sin, sin], axis=1), (1, HEADS_PER_TILE))
    return cos_t, sin_t


def kernel(x, positions, w_in, w_s, b_s, ln_v_g, ln_v_b, w_proj_a, w_proj_b, w_out,
           g_mix_pre, g_mix_post, g_ffn_pre, g_ffn_post, w_gate_up, w_down):
    b, s, d = x.shape
    depth = w_in.shape[0]
    t = b * s
    cos_t, sin_t = _rope_tables(positions)
    xf = x.reshape(t, d)
    gains = lambda g: g.reshape(depth, 1, -1)
    ws = w_s.reshape(depth, GM_GROUPS // 2, 2, GM_CHUNK, GM_CHUNK).transpose(0, 1, 3, 2, 4)
    ws = ws.reshape(depth, GM_GROUPS // 2, GM_CHUNK, 2 * GM_CHUNK).astype(BF16)
    bmat = jnp.repeat(b_s.transpose(0, 2, 1), GM_GROUP_DIM, axis=2)
    LATE_KEYS = ("mix", "gu", "d", "pa", "pb", "o")
    ALL_KEYS = ("mix", "qkv", "gu", "d", "pa", "pb", "o")

    def attn_casts(layer, late_only):
        whole = lambda stack, rep=1: (stack, layer, (None,), rep)
        in_dests = (MIX_COLS,) if late_only else (MIX_COLS, QKV_COLS)
        return [(w_in, layer, in_dests, 1), whole(w_gate_up), whole(w_down, 2), whole(w_proj_a),
                whole(w_proj_b), whole(w_out)]

    weights = {0: {"qkv": w_in[0][:, QKV_LO:GATE_LO].astype(BF16)}}
    qkv, _ = _qkv_proj(xf, gains(g_mix_pre), weights[0]["qkv"], cos_t, sin_t, 0, tm=1024)
    for l in range(depth):
        nxt = l + 1 < depth
        casts = (attn_casts(0, True) if l == 0 else []) + (attn_casts(l + 1, False) if nxt else [])
        q, k, va = (a.reshape(b, s, -1) for a in qkv)
        yb, cast_out = _moba(q, k, va, casts)
        cast_out = list(cast_out)
        if l == 0:
            weights[0].update(zip(LATE_KEYS, cast_out[:len(LATE_KEYS)]))
            cast_out = cast_out[len(LATE_KEYS):]
        if nxt:
            weights[l + 1] = dict(zip(ALL_KEYS, cast_out))
        w = weights.pop(l)
        params = [(gains(g_mix_pre), True), (w["mix"], False), (gains(ln_v_g), True), (gains(ln_v_b), True),
                  (ws, True), (bmat, True), (w["pa"], False), (w["pb"], False), (w["o"], False),
                  (gains(g_mix_post), True), (gains(g_ffn_pre), True), (w["gu"], False), (w["d"], False),
                  (gains(g_ffn_post), True)]
        next_qkv = (gains(g_mix_pre), weights[l + 1]["qkv"], cos_t, sin_t) if nxt else None
        xf, qkv, _ = _mix_ffn(xf, yb.reshape(t, -1), params, l, tm=512, next_qkv=next_qkv)
    return xf.reshape(b, s, d)
```

```python
import jax
import jax.numpy as jnp
from jax import lax
from jax.experimental import pallas as pl
from jax.experimental.pallas import tpu as pltpu

D_MODEL = 1024
GM_GROUPS = 8
GM_GROUP_DIM = 64
GM_WIDTH = GM_GROUPS * GM_GROUP_DIM
GM_CHUNK = 128
ATTN_HEADS = 8
HEAD_DIM = 64
ATTN_WIDTH = ATTN_HEADS * HEAD_DIM
MOBA_BLOCK = 256
MOBA_TOPK = 3
ROPE_THETA = 10000.0
D_FF = 2816
IN_WIDTH = 2 * GM_WIDTH + 3 * ATTN_WIDTH + 2 * D_MODEL
NORM_EPS = 1e-6
NEG_INF = -1e30
LOG2_E = 1.4426950408889634

QKV_LO = 2 * GM_WIDTH
GATE_LO = QKV_LO + 3 * ATTN_WIDTH
MIX_COLS = ((0, QKV_LO), (GATE_LO, IN_WIDTH))
QKV_COLS = ((QKV_LO, GATE_LO),)

LANES = 128
HEADS_PER_TILE = LANES // HEAD_DIM
VMEM_LIMIT_BYTES = 56 * 1024 * 1024
MXU_TILE = 256

MIX_FFN_TILE = 512
QKV_TILE = 1024

TOKEN_SUB = 256
FFN_CHUNKS = ((0, 5 * MXU_TILE), (5 * MXU_TILE, D_FF - 5 * MXU_TILE))

F32 = jnp.float32
BF16 = jnp.bfloat16


def _rms(x, g):
    return x * lax.rsqrt(jnp.mean(x * x, axis=-1, keepdims=True) + NORM_EPS) * g


def _layer_spec(param, l):
    if l is None:
        return pl.BlockSpec(param.shape, lambda *_: (0,) * param.ndim, pipeline_mode=pl.Buffered(1))
    shape = param.shape[1:]
    return pl.BlockSpec((None,) + shape, lambda *_: (l,) + (0,) * len(shape),
                        pipeline_mode=pl.Buffered(1))


def _cast_plan(items, steps):
    in_specs, out_specs, out_shapes, dests_per_item = [], [], [], []
    for w, layer, dests, rep in items:
        _, rows, cols = w.shape
        rb = rows * rep // steps
        in_specs.append(pl.BlockSpec((None, rb, cols), lambda i, layer=layer, rep=rep: (layer, i // rep, 0)))
        dests = tuple(((0, cols),) if d is None else d for d in dests)
        for d in dests:
            width = sum(hi - lo for lo, hi in d)
            out_specs.append(pl.BlockSpec((rb, width), lambda i, rep=rep: (i // rep, 0)))
            out_shapes.append(jax.ShapeDtypeStruct((rows, width), BF16))
        dests_per_item.append(dests)
    return in_specs, out_specs, out_shapes, dests_per_item


def _with_casts(body, n_in, n_out, dests_per_item):
    n_src = len(dests_per_item)
    n_dst = sum(len(d) for d in dests_per_item)

    def kernel_fn(*refs):
        ins, rest = refs[:n_in], refs[n_in:]
        srcs, rest = rest[:n_src], rest[n_src:]
        outs, rest = rest[:n_out], rest[n_out:]
        dsts, scratch = list(rest[:n_dst]), rest[n_dst:]
        for src, dests in zip(srcs, dests_per_item):
            for ranges in dests:
                dst = dsts.pop(0)
                off = 0
                for lo, hi in ranges:
                    dst[:, off:off + hi - lo] = src[:, lo:hi].astype(BF16)
                    off += hi - lo
        body(*ins, *outs, *scratch)
    return kernel_fn


def _qkv_sections(cos_ref, sin_ref, q_ref, k_ref, va_ref):
    first_half = (lax.broadcasted_iota(jnp.int32, (TOKEN_SUB, LANES), 1) % HEAD_DIM) < HEAD_DIM // 2

    def rope_out(out_ref, scale):
        def ep(z, rows):
            cos = cos_ref[rows, :]
            sin = sin_ref[rows, :]
            for c in range(ATTN_WIDTH // LANES):
                zc = z[:, c * LANES:(c + 1) * LANES]
                partner = jnp.where(first_half,
                                    pltpu.roll(zc, LANES - HEAD_DIM // 2, 1),
                                    pltpu.roll(zc, HEAD_DIM // 2, 1))
                r = zc * cos + partner * sin
                if scale != 1.0:
                    r = r * scale
                out_ref[rows, c * LANES:(c + 1) * LANES] = r.astype(BF16)
        return ep

    def cast_out(z, rows):
        va_ref[rows, :] = z.astype(BF16)

    return [
        (0, rope_out(q_ref, HEAD_DIM ** -0.5 * LOG2_E)),
        (ATTN_WIDTH, rope_out(k_ref, 1.0)),
        (2 * ATTN_WIDTH, cast_out),
    ]


def _qkv_proj_kernel(x_ref, g_ref, w_ref, cos_ref, sin_ref, q_ref, k_ref, va_ref):
    tm = x_ref.shape[0]
    sections = _qkv_sections(cos_ref, sin_ref, q_ref, k_ref, va_ref)
    items = [(r, sec) for r in range(tm // TOKEN_SUB) for sec in sections]

    h = {}
    pending = None
    for r, (lo, ep) in items:
        rows = slice(r * TOKEN_SUB, (r + 1) * TOKEN_SUB)
        if r not in h:
            h[r] = _rms(x_ref[rows, :], g_ref[...]).astype(BF16)
        z = jnp.dot(h[r], w_ref[:, lo:lo + ATTN_WIDTH], preferred_element_type=F32)
        if pending is not None:
            pending[0](pending[1], pending[2])
        pending = (ep, z, rows)
    pending[0](pending[1], pending[2])


def _qkv_proj(x, g, w_qkv, cos, sin, l, tm, casts=()):
    t = x.shape[0]
    steps = t // tm
    row = lambda width: pl.BlockSpec((tm, width), lambda i: (i, 0))
    c_in, c_out, c_shape, dests = _cast_plan(casts, steps)
    outs = pl.pallas_call(
        _with_casts(_qkv_proj_kernel, 5, 3, dests),
        grid=(steps,),
        in_specs=[row(D_MODEL), _layer_spec(g, l), _layer_spec(w_qkv, None), row(LANES), row(LANES)] + c_in,
        out_specs=[row(ATTN_WIDTH)] * 3 + c_out,
        out_shape=[jax.ShapeDtypeStruct((t, ATTN_WIDTH), BF16)] * 3 + c_shape,
        compiler_params=pltpu.CompilerParams(dimension_semantics=("arbitrary",),
                                             vmem_limit_bytes=VMEM_LIMIT_BYTES),
        name="qkv_proj",
    )(x, g, w_qkv, cos, sin, *[c[0] for c in casts])
    return outs[:3], outs[3:]


def _moba_kernel(q_ref, k_ref, v_ref, o_ref, vt_ref, kbar_ref, s0_ref, s1_ref, p0_ref, p1_ref):
    nb = k_ref.shape[1] // MOBA_BLOCK
    lane = lax.broadcasted_iota(jnp.int32, (1, LANES), 1)
    key_id = lax.broadcasted_iota(jnp.int32, (MOBA_BLOCK, MOBA_BLOCK), 0)
    qry_id = lax.broadcasted_iota(jnp.int32, (MOBA_BLOCK, MOBA_BLOCK), 1)
    causal = key_id <= qry_id
    jidx = lax.broadcasted_iota(jnp.int32, (nb, MOBA_BLOCK), 0)
    drow = lax.broadcasted_iota(jnp.int32, (LANES, 1), 0)
    nt = (((1,), (1,)), ((), ()))

    vt = v_ref[0].astype(F32).T
    for hh in range(HEADS_PER_TILE):
        vt_ref[hh] = jnp.where((drow // HEAD_DIM) == hh, vt, 1.0).astype(BF16)
    for j in range(nb):
        kb = k_ref[0, j * MOBA_BLOCK:(j + 1) * MOBA_BLOCK, :]
        kbar = jnp.mean(kb.astype(F32), axis=0, keepdims=True)
        hi = kbar.astype(BF16)
        lo = (kbar - hi.astype(F32)).astype(BF16)
        kbar_ref[j:j + 1, :] = hi.astype(F32)
        kbar_ref[nb + j:nb + j + 1, :] = lo.astype(F32)
    kbar2 = kbar_ref[...].astype(BF16)

    def scores(n, hh):
        past = n * MOBA_BLOCK
        q = q_ref[0, past:past + MOBA_BLOCK, :]
        own = (lane // HEAD_DIM) == hh
        q_own = jnp.where(own, q, jnp.zeros_like(q))
        if n > MOBA_TOPK:
            g2 = lax.dot_general(kbar2, q_own, nt, preferred_element_type=F32)
            gate = jnp.where(jidx < n, g2[:nb] + g2[nb:], -jnp.inf)
            cnt = jnp.zeros((nb, MOBA_BLOCK), jnp.int32)
            for jp in range(n):
                gj = gate[jp:jp + 1, :]
                ahead = (gj > gate) | ((gj == gate) & (jp < jidx))
                cnt = cnt + jnp.where(ahead, 1, 0)
            bias = jnp.where(cnt < MOBA_TOPK, 0.0, NEG_INF)
        else:
            bias = None
        return {"n": n, "hh": hh, "q": q_own, "bias": bias, "m": None}

    def block(j):
        return slice(j * MOBA_BLOCK, (j + 1) * MOBA_BLOCK)

    s_refs, p_refs = (s0_ref, s1_ref), (p0_ref, p1_ref)
    dyn0 = lax.shift_right_logical(pl.program_id(0), 30)

    def score_block(it, slot, j):
        n, bias = it["n"], it["bias"]
        sj = lax.dot_general(k_ref[0, block(j), :], it["q"], nt, preferred_element_type=F32)
        if j == n:
            sj = jnp.where(causal, sj, NEG_INF)
        s_refs[slot][dyn0, block(j), :] = sj
        mj = jnp.max(sj, axis=0, keepdims=True)
        if bias is not None and j < n:
            mj = mj + bias[j:j + 1, :]
        it["m"] = mj if it["m"] is None else jnp.maximum(it["m"], mj)

    def prob_block(it, slot, j):
        n, bias, m = it["n"], it["bias"], it["m"]
        shift = m if (bias is None or j == n) else m - bias[j:j + 1, :]
        p_refs[slot][dyn0, block(j), :] = jnp.exp2(s_refs[slot][dyn0, block(j), :] - shift).astype(BF16)

    def output(it, slot):
        keys = (it["n"] + 1) * MOBA_BLOCK
        acc = jnp.dot(vt_ref[it["hh"], :, :keys], p_refs[slot][dyn0, :keys, :], preferred_element_type=F32)
        ones_row = HEAD_DIM * (1 - it["hh"])
        return acc / acc[ones_row:ones_row + 1, :]

    order = [(n, hh) for n in reversed(range(nb)) for hh in range(HEADS_PER_TILE)]
    items, outs = {}, {}
    for i in range(len(order) + 2):
        if i < len(order):
            items[i] = scores(*order[i])
        for j in range(nb):
            if i in items and j <= items[i]["n"]:
                score_block(items[i], i % 2, j)
            if i - 1 in items and j <= items[i - 1]["n"]:
                prob_block(items[i - 1], (i - 1) % 2, j)
        if i - 2 in items:
            it = items.pop(i - 2)
            outs[it["hh"]] = output(it, i % 2)
            if it["hh"] == HEADS_PER_TILE - 1:
                out_t = jnp.where(drow < HEAD_DIM, outs[0], outs[1])
                o_ref[0, block(it["n"]), :] = out_t.T.astype(BF16)


def _moba(q, k, v, casts=()):
    b, s, w = q.shape
    tiles = w // LANES
    nb = s // MOBA_BLOCK
    steps = b * tiles
    spec = pl.BlockSpec((1, s, LANES), lambda bp: (bp // tiles, 0, bp % tiles))
    c_in, c_out, c_shape, dests = _cast_plan(casts, steps)
    outs = pl.pallas_call(
        _with_casts(_moba_kernel, 3, 1, dests),
        grid=(steps,),
        in_specs=[spec, spec, spec] + c_in,
        out_specs=[spec] + c_out,
        out_shape=[jax.ShapeDtypeStruct((b, s, w), BF16)] + c_shape,
        scratch_shapes=[pltpu.VMEM((HEADS_PER_TILE, LANES, s), BF16),
                        pltpu.VMEM((2 * nb, LANES), F32),
                        pltpu.VMEM((1, s, MOBA_BLOCK), F32), pltpu.VMEM((1, s, MOBA_BLOCK), F32),
                        pltpu.VMEM((1, s, MOBA_BLOCK), BF16), pltpu.VMEM((1, s, MOBA_BLOCK), BF16)],
        compiler_params=pltpu.CompilerParams(dimension_semantics=("arbitrary",),
                                             vmem_limit_bytes=VMEM_LIMIT_BYTES),
        name="moba",
    )(q, k, v, *[c[0] for c in casts])
    return outs[0], outs[1:]


def _mix_ffn_kernel(x_ref, yb_ref, gin_ref, wmix_ref, lng_ref, lnb_ref, ws_ref, bmat_ref,
                    wpa_ref, wpb_ref, wo_ref, gmix_ref, gpre_ref, wgu_ref, wd_ref, gffn_ref,
                    *rest):
    next_qkv = len(rest) > 1
    o_ref = rest[4] if next_qkv else rest[0]
    tm = x_ref.shape[0]
    lane = lax.broadcasted_iota(jnp.int32, (1, LANES), 1)
    first = lane < GM_GROUP_DIM
    zero = jnp.zeros((GM_CHUNK, LANES), BF16)
    wi = lax.broadcasted_iota(jnp.int32, (GM_CHUNK, 2 * GM_CHUNK), 0)
    wj = lax.broadcasted_iota(jnp.int32, (GM_CHUNK, 2 * GM_CHUNK), 1) % GM_CHUNK
    tril = wj <= wi
    bmat = bmat_ref[...]
    w_tril = [jnp.where(tril, ws_ref[gp], jnp.zeros_like(ws_ref[gp])) for gp in range(GM_WIDTH // LANES)]
    n_chunks = TOKEN_SUB // GM_CHUNK

    def rows_of(r):
        return slice(r * TOKEN_SUB, (r + 1) * TOKEN_SUB)

    def proj(key, lo, width):
        def stage(r, st):
            if "h0" not in st:
                st["h0"] = _rms(x_ref[rows_of(r), :], gin_ref[...]).astype(BF16)
            st[key] = jnp.dot(st["h0"], wmix_ref[:, lo:lo + width], preferred_element_type=F32)
        return stage

    def gelu_u(r, st):
        st["u"] = jax.nn.gelu(st.pop("zu"))

    def gelu_ln_v(r, st):
        v = jax.nn.gelu(st.pop("zv"))
        mu = jnp.mean(v, axis=-1, keepdims=True)
        vc = v - mu
        var = jnp.mean(vc * vc, axis=-1, keepdims=True)
        st["v"] = (vc * lax.rsqrt(var + NORM_EPS) * lng_ref[...] + lnb_ref[...]).astype(BF16)

    def gating(r, st):
        v = st.pop("v")
        u = st.pop("u")
        cols = []
        for gp in range(GM_WIDTH // LANES):
            stacked = []
            for c in range(n_chunks):
                vp = v[c * GM_CHUNK:(c + 1) * GM_CHUNK, gp * LANES:(gp + 1) * LANES]
                stacked.append(jnp.concatenate([jnp.where(first, vp, zero), jnp.where(first, zero, vp)],
                                               axis=0))
            cols.append(jnp.dot(w_tril[gp], jnp.concatenate(stacked, axis=1), preferred_element_type=F32))
        ya = []
        for c in range(n_chunks):
            mixed = jnp.concatenate([col[:, c * LANES:(c + 1) * LANES] for col in cols], axis=1) + bmat
            ya.append((u[c * GM_CHUNK:(c + 1) * GM_CHUNK] * mixed).astype(BF16))
        st["ya"] = jnp.concatenate(ya, axis=0)

    def branch(key, src, w_ref, gate_key):
        def stage(r, st):
            operand = st.pop("ya") if src == "ya" else yb_ref[rows_of(r), :]
            p = jnp.dot(operand, w_ref[...], preferred_element_type=F32)
            st[key] = jax.nn.sigmoid(st.pop(gate_key)) * p
        return stage

    def project(r, st):
        merged = (st.pop("ma") + st.pop("mb")).astype(BF16)
        st.pop("h0")
        y = jnp.dot(merged, wo_ref[...], preferred_element_type=F32)
        x1 = x_ref[rows_of(r), :] + _rms(y, gmix_ref[...])
        st["x1"] = x1
        st["h"] = _rms(x1, gpre_ref[...]).astype(BF16)

    def up(c):
        lo, width = FFN_CHUNKS[c]

        def stage(r, st):
            st["gate"] = jnp.dot(st["h"], wgu_ref[:, lo:lo + width], preferred_element_type=F32)
            st["lin"] = jnp.dot(st["h"], wgu_ref[:, D_FF + lo:D_FF + lo + width], preferred_element_type=F32)
        return stage

    def down(c):
        lo, width = FFN_CHUNKS[c]

        def stage(r, st):
            act = (jax.nn.silu(st.pop("gate")) * st.pop("lin")).astype(BF16)
            part = jnp.dot(act, wd_ref[lo:lo + width, :], preferred_element_type=F32)
            st["f"] = part if c == 0 else st["f"] + part
        return stage

    def finish(r, st):
        x2 = st.pop("x1") + _rms(st.pop("f"), gffn_ref[...])
        o_ref[rows_of(r), :] = x2
        if next_qkv:
            st["hq"] = _rms(x2, rest[0][...]).astype(BF16)

    stages = [proj("zu", 0, GM_WIDTH), proj("zv", GM_WIDTH, GM_WIDTH), gelu_u, gelu_ln_v,
              proj("zga", 2 * GM_WIDTH, D_MODEL), gating, proj("zgb", 2 * GM_WIDTH + D_MODEL, D_MODEL),
              branch("ma", "ya", wpa_ref, "zga"), branch("mb", "yb", wpb_ref, "zgb"), project]
    for c in range(len(FFN_CHUNKS)):
        stages += [up(c), down(c)]
    stages.append(finish)
    if next_qkv:
        wqkv_ref = rest[1]

        def qkv_section(lo, ep):
            def stage(r, st):
                ep(jnp.dot(st["hq"], wqkv_ref[:, lo:lo + ATTN_WIDTH], preferred_element_type=F32), rows_of(r))
            return stage
        stages += [qkv_section(lo, ep) for lo, ep in _qkv_sections(rest[2], rest[3], *rest[5:8])]

    state = [{} for _ in range(tm // TOKEN_SUB)]
    for stage in stages:
        for r, st in enumerate(state):
            stage(r, st)


def _mix_ffn(x, yb, params, l, tm, casts=(), next_qkv=None):
    t = x.shape[0]
    steps = t // tm
    row = lambda width: pl.BlockSpec((tm, width), lambda i: (i, 0))
    c_in, c_out, c_shape, dests = _cast_plan(casts, steps)
    operands = [x, yb] + [p for p, _ in params]
    in_specs = [row(D_MODEL), row(ATTN_WIDTH)] + [_layer_spec(p, l if stacked else None) for p, stacked in params]
    out_specs = [row(D_MODEL)]
    out_shape = [jax.ShapeDtypeStruct((t, D_MODEL), F32)]
    if next_qkv is not None:
        g_next, w_qkv, cos, sin = next_qkv
        operands += [g_next, w_qkv, cos, sin]
        in_specs += [_layer_spec(g_next, l + 1), _layer_spec(w_qkv, None), row(LANES), row(LANES)]
        out_specs += [row(ATTN_WIDTH)] * 3
        out_shape += [jax.ShapeDtypeStruct((t, ATTN_WIDTH), BF16)] * 3
    n_out = len(out_shape)
    outs = pl.pallas_call(
        _with_casts(_mix_ffn_kernel, len(operands), n_out, dests),
        grid=(steps,),
        in_specs=in_specs + c_in,
        out_specs=out_specs + c_out,
        out_shape=out_shape + c_shape,
        compiler_params=pltpu.CompilerParams(dimension_semantics=("arbitrary",),
                                             vmem_limit_bytes=VMEM_LIMIT_BYTES),
        name="mix_ffn",
    )(*operands, *[c[0] for c in casts])
    return outs[0], (tuple(outs[1:4]) if next_qkv is not None else None), outs[n_out:]


def _rope_tables(positions):
    half = HEAD_DIM // 2
    inv_freq = 1.0 / (ROPE_THETA ** (jnp.arange(0, HEAD_DIM, 2, dtype=F32) / HEAD_DIM))
    ang = positions.astype(F32).reshape(-1, 1) * inv_freq
    cos = jnp.cos(ang)
    sin = jnp.sin(ang)
    cos_t = jnp.tile(cos, (1, LANES // half))
    sin_t = jnp.tile(jnp.concatenate([-sin, sin], axis=1), (1, HEADS_PER_TILE))
    return cos_t, sin_t


def kernel(x, positions, w_in, w_s, b_s, ln_v_g, ln_v_b, w_proj_a, w_proj_b, w_out,
           g_mix_pre, g_mix_post, g_ffn_pre, g_ffn_post, w_gate_up, w_down):
    b, s, d = x.shape
    depth = w_in.shape[0]
    t = b * s
    cos_t, sin_t = _rope_tables(positions)
    xf = x.reshape(t, d)
    gains = lambda g: g.reshape(depth, 1, -1)
    ws = w_s.reshape(depth, GM_GROUPS // 2, 2, GM_CHUNK, GM_CHUNK).transpose(0, 1, 3, 2, 4)
    ws = ws.reshape(depth, GM_GROUPS // 2, GM_CHUNK, 2 * GM_CHUNK).astype(BF16)
    bmat = jnp.repeat(b_s.transpose(0, 2, 1), GM_GROUP_DIM, axis=2)
    LATE_KEYS = ("mix", "gu", "d", "pa", "pb", "o")
    ALL_KEYS = ("mix", "qkv", "gu", "d", "pa", "pb", "o")

    def attn_casts(layer, late_only):
        whole = lambda stack, rep=1: (stack, layer, (None,), rep)
        in_dests = (MIX_COLS,) if late_only else (MIX_COLS, QKV_COLS)
        return [(w_in, layer, in_dests, 1), whole(w_gate_up), whole(w_down, 2), whole(w_proj_a),
                whole(w_proj_b), whole(w_out)]

    weights = {0: {"qkv": w_in[0][:, QKV_LO:GATE_LO].astype(BF16)}}
    qkv, _ = _qkv_proj(xf, gains(g_mix_pre), weights[0]["qkv"], cos_t, sin_t, 0, tm=QKV_TILE)
    for l in range(depth):
        nxt = l + 1 < depth
        casts = (attn_casts(0, True) if l == 0 else []) + (attn_casts(l + 1, False) if nxt else [])
        q, k, va = (a.reshape(b, s, -1) for a in qkv)
        yb, cast_out = _moba(q, k, va, casts)
        cast_out = list(cast_out)
        if l == 0:
            weights[0].update(zip(LATE_KEYS, cast_out[:len(LATE_KEYS)]))
            cast_out = cast_out[len(LATE_KEYS):]
        if nxt:
            weights[l + 1] = dict(zip(ALL_KEYS, cast_out))
        w = weights.pop(l)
        params = [(gains(g_mix_pre), True), (w["mix"], False), (gains(ln_v_g), True), (gains(ln_v_b), True),
                  (ws, True), (bmat, True), (w["pa"], False), (w["pb"], False), (w["o"], False),
                  (gains(g_mix_post), True), (gains(g_ffn_pre), True), (w["gu"], False), (w["d"], False),
                  (gains(g_ffn_post), True)]
        next_qkv = (gains(g_mix_pre), weights[l + 1]["qkv"], cos_t, sin_t) if nxt else None
        xf, qkv, _ = _mix_ffn(xf, yb.reshape(t, -1), params, l, tm=MIX_FFN_TILE, next_qkv=next_qkv)
    return xf.reshape(b, s, d)
```

```python
import jax
import jax.numpy as jnp
from jax import lax
from jax.experimental import pallas as pl
from jax.experimental.pallas import tpu as pltpu

D_MODEL = 1024
GM_GROUPS = 8
GM_GROUP_DIM = 64
GM_WIDTH = GM_GROUPS * GM_GROUP_DIM
GM_CHUNK = 128
ATTN_HEADS = 8
HEAD_DIM = 64
ATTN_WIDTH = ATTN_HEADS * HEAD_DIM
MOBA_BLOCK = 256
MOBA_TOPK = 3
ROPE_THETA = 10000.0
D_FF = 2816
IN_WIDTH = 2 * GM_WIDTH + 3 * ATTN_WIDTH + 2 * D_MODEL
NORM_EPS = 1e-6
NEG_INF = -1e30
LOG2_E = 1.4426950408889634

QKV_LO = 2 * GM_WIDTH
GATE_LO = QKV_LO + 3 * ATTN_WIDTH
MIX_COLS = ((0, QKV_LO), (GATE_LO, IN_WIDTH))
QKV_COLS = ((QKV_LO, GATE_LO),)

LANES = 128
HEADS_PER_TILE = LANES // HEAD_DIM
VMEM_LIMIT_BYTES = 56 * 1024 * 1024
MXU_TILE = 256

MIX_FFN_TILE = 512
QKV_TILE = 1024

TOKEN_SUB = 256
FFN_CHUNKS = ((0, 5 * MXU_TILE), (5 * MXU_TILE, D_FF - 5 * MXU_TILE))

F32 = jnp.float32
BF16 = jnp.bfloat16


def _rms(x, g):
    return x * lax.rsqrt(jnp.mean(x * x, axis=-1, keepdims=True) + NORM_EPS) * g


def _layer_spec(param, l):
    if l is None:
        return pl.BlockSpec(param.shape, lambda *_: (0,) * param.ndim, pipeline_mode=pl.Buffered(1))
    shape = param.shape[1:]
    return pl.BlockSpec((None,) + shape, lambda *_: (l,) + (0,) * len(shape),
                        pipeline_mode=pl.Buffered(1))


def _cast_plan(items, steps):
    in_specs, out_specs, out_shapes, dests_per_item = [], [], [], []
    for w, layer, dests, rep in items:
        _, rows, cols = w.shape
        rb = rows * rep // steps
        in_specs.append(pl.BlockSpec((None, rb, cols), lambda i, layer=layer, rep=rep: (layer, i // rep, 0)))
        dests = tuple(((0, cols),) if d is None else d for d in dests)
        for d in dests:
            width = sum(hi - lo for lo, hi in d)
            out_specs.append(pl.BlockSpec((rb, width), lambda i, rep=rep: (i // rep, 0)))
            out_shapes.append(jax.ShapeDtypeStruct((rows, width), BF16))
        dests_per_item.append(dests)
    return in_specs, out_specs, out_shapes, dests_per_item


def _with_casts(body, n_in, n_out, dests_per_item):
    n_src = len(dests_per_item)
    n_dst = sum(len(d) for d in dests_per_item)

    def kernel_fn(*refs):
        ins, rest = refs[:n_in], refs[n_in:]
        srcs, rest = rest[:n_src], rest[n_src:]
        outs, rest = rest[:n_out], rest[n_out:]
        dsts, scratch = list(rest[:n_dst]), rest[n_dst:]
        for src, dests in zip(srcs, dests_per_item):
            for ranges in dests:
                dst = dsts.pop(0)
                off = 0
                for lo, hi in ranges:
                    dst[:, off:off + hi - lo] = src[:, lo:hi].astype(BF16)
                    off += hi - lo
        body(*ins, *outs, *scratch)
    return kernel_fn


def _qkv_sections(cos_ref, sin_ref, q_ref, k_ref, va_ref):
    first_half = (lax.broadcasted_iota(jnp.int32, (TOKEN_SUB, LANES), 1) % HEAD_DIM) < HEAD_DIM // 2

    def rope_out(out_ref, scale):
        def ep(z, rows):
            cos = cos_ref[rows, :]
            sin = sin_ref[rows, :]
            for c in range(ATTN_WIDTH // LANES):
                zc = z[:, c * LANES:(c + 1) * LANES]
                partner = jnp.where(first_half,
                                    pltpu.roll(zc, LANES - HEAD_DIM // 2, 1),
                                    pltpu.roll(zc, HEAD_DIM // 2, 1))
                r = zc * cos + partner * sin
                if scale != 1.0:
                    r = r * scale
                out_ref[rows, c * LANES:(c + 1) * LANES] = r.astype(BF16)
        return ep

    def cast_out(z, rows):
        va_ref[rows, :] = z.astype(BF16)

    return [
        (0, rope_out(q_ref, HEAD_DIM ** -0.5 * LOG2_E)),
        (ATTN_WIDTH, rope_out(k_ref, 1.0)),
        (2 * ATTN_WIDTH, cast_out),
    ]


def _qkv_proj_kernel(x_ref, g_ref, w_ref, cos_ref, sin_ref, q_ref, k_ref, va_ref):
    tm = x_ref.shape[0]
    sections = _qkv_sections(cos_ref, sin_ref, q_ref, k_ref, va_ref)
    items = [(r, sec) for r in range(tm // TOKEN_SUB) for sec in sections]

    h = {}
    pending = None
    for r, (lo, ep) in items:
        rows = slice(r * TOKEN_SUB, (r + 1) * TOKEN_SUB)
        if r not in h:
            h[r] = _rms(x_ref[rows, :], g_ref[...]).astype(BF16)
        z = jnp.dot(h[r], w_ref[:, lo:lo + ATTN_WIDTH], preferred_element_type=F32)
        if pending is not None:
            pending[0](pending[1], pending[2])
        pending = (ep, z, rows)
    pending[0](pending[1], pending[2])


def _qkv_proj(x, g, w_qkv, cos, sin, l, tm, casts=()):
    t = x.shape[0]
    steps = t // tm
    row = lambda width: pl.BlockSpec((tm, width), lambda i: (i, 0))
    c_in, c_out, c_shape, dests = _cast_plan(casts, steps)
    outs = pl.pallas_call(
        _with_casts(_qkv_proj_kernel, 5, 3, dests),
        grid=(steps,),
        in_specs=[row(D_MODEL), _layer_spec(g, l), _layer_spec(w_qkv, None), row(LANES), row(LANES)] + c_in,
        out_specs=[row(ATTN_WIDTH)] * 3 + c_out,
        out_shape=[jax.ShapeDtypeStruct((t, ATTN_WIDTH), BF16)] * 3 + c_shape,
        compiler_params=pltpu.CompilerParams(dimension_semantics=("arbitrary",),
                                             vmem_limit_bytes=VMEM_LIMIT_BYTES),
        name="qkv_proj",
    )(x, g, w_qkv, cos, sin, *[c[0] for c in casts])
    return outs[:3], outs[3:]


def _moba_kernel(q_ref, k_ref, v_ref, o_ref, vt_ref, kbar_ref, s0_ref, s1_ref, p0_ref, p1_ref, p2_ref):
    nb = k_ref.shape[1] // MOBA_BLOCK
    lane = lax.broadcasted_iota(jnp.int32, (1, LANES), 1)
    key_id = lax.broadcasted_iota(jnp.int32, (MOBA_BLOCK, MOBA_BLOCK), 0)
    qry_id = lax.broadcasted_iota(jnp.int32, (MOBA_BLOCK, MOBA_BLOCK), 1)
    causal = key_id <= qry_id
    jidx = lax.broadcasted_iota(jnp.int32, (nb, MOBA_BLOCK), 0)
    drow = lax.broadcasted_iota(jnp.int32, (LANES, 1), 0)
    nt = (((1,), (1,)), ((), ()))

    vt = v_ref[0].astype(F32).T
    for hh in range(HEADS_PER_TILE):
        vt_ref[hh] = jnp.where((drow // HEAD_DIM) == hh, vt, 1.0).astype(BF16)
    for j in range(nb):
        kb = k_ref[0, j * MOBA_BLOCK:(j + 1) * MOBA_BLOCK, :]
        kbar = jnp.mean(kb.astype(F32), axis=0, keepdims=True)
        hi = kbar.astype(BF16)
        lo = (kbar - hi.astype(F32)).astype(BF16)
        kbar_ref[j:j + 1, :] = hi.astype(F32)
        kbar_ref[nb + j:nb + j + 1, :] = lo.astype(F32)
    kbar2 = kbar_ref[...].astype(BF16)

    def scores(n, hh):
        past = n * MOBA_BLOCK
        q = q_ref[0, past:past + MOBA_BLOCK, :]
        own = (lane // HEAD_DIM) == hh
        q_own = jnp.where(own, q, jnp.zeros_like(q))
        if n > MOBA_TOPK:
            g2 = lax.dot_general(kbar2, q_own, nt, preferred_element_type=F32)
            gate = jnp.where(jidx < n, g2[:nb] + g2[nb:], -jnp.inf)
            cnt = jnp.zeros((nb, MOBA_BLOCK), jnp.int32)
            for jp in range(n):
                gj = gate[jp:jp + 1, :]
                ahead = (gj > gate) | ((gj == gate) & (jp < jidx))
                cnt = cnt + jnp.where(ahead, 1, 0)
            bias = jnp.where(cnt < MOBA_TOPK, 0.0, NEG_INF)
        else:
            bias = None
        return {"n": n, "hh": hh, "q": q_own, "bias": bias, "m": None}

    def block(j):
        return slice(j * MOBA_BLOCK, (j + 1) * MOBA_BLOCK)

    s_refs, p_refs = (s0_ref, s1_ref), (p0_ref, p1_ref, p2_ref)
    dyn0 = lax.shift_right_logical(pl.program_id(0), 30)

    def score_block(it, slot, j):
        n, bias = it["n"], it["bias"]
        sj = lax.dot_general(k_ref[0, block(j), :], it["q"], nt, preferred_element_type=F32)
        if j == n:
            sj = jnp.where(causal, sj, NEG_INF)
        s_refs[slot][dyn0, block(j), :] = sj
        mj = jnp.max(sj, axis=0, keepdims=True)
        if bias is not None and j < n:
            mj = mj + bias[j:j + 1, :]
        it["m"] = mj if it["m"] is None else jnp.maximum(it["m"], mj)

    def prob_block(it, slot, pslot, j):
        n, bias, m = it["n"], it["bias"], it["m"]
        shift = m if (bias is None or j == n) else m - bias[j:j + 1, :]
        p_refs[pslot][dyn0, block(j), :] = jnp.exp2(s_refs[slot][dyn0, block(j), :] - shift).astype(BF16)

    def output(it, slot):
        keys = (it["n"] + 1) * MOBA_BLOCK
        acc = jnp.dot(vt_ref[it["hh"], :, :keys], p_refs[slot][dyn0, :keys, :], preferred_element_type=F32)
        ones_row = HEAD_DIM * (1 - it["hh"])
        return acc / acc[ones_row:ones_row + 1, :]

    order = [(n, hh) for n in reversed(range(nb)) for hh in range(HEADS_PER_TILE)]
    items, outs = {}, {}
    for i in range(len(order) + 3):
        if i < len(order):
            items[i] = scores(*order[i])
        for j in range(nb):
            if i in items and j <= items[i]["n"]:
                score_block(items[i], i % 2, j)
            if i - 1 in items and j <= items[i - 1]["n"]:
                prob_block(items[i - 1], (i - 1) % 2, (i - 1) % 3, j)
        if i - 3 in items:
            it = items.pop(i - 3)
            outs[it["hh"]] = output(it, (i - 3) % 3)
            if it["hh"] == HEADS_PER_TILE - 1:
                out_t = jnp.where(drow < HEAD_DIM, outs[0], outs[1])
                o_ref[0, block(it["n"]), :] = out_t.T.astype(BF16)


def _moba(q, k, v, casts=()):
    b, s, w = q.shape
    tiles = w // LANES
    nb = s // MOBA_BLOCK
    steps = b * tiles
    spec = pl.BlockSpec((1, s, LANES), lambda bp: (bp // tiles, 0, bp % tiles))
    c_in, c_out, c_shape, dests = _cast_plan(casts, steps)
    outs = pl.pallas_call(
        _with_casts(_moba_kernel, 3, 1, dests),
        grid=(steps,),
        in_specs=[spec, spec, spec] + c_in,
        out_specs=[spec] + c_out,
        out_shape=[jax.ShapeDtypeStruct((b, s, w), BF16)] + c_shape,
        scratch_shapes=[pltpu.VMEM((HEADS_PER_TILE, LANES, s), BF16),
                        pltpu.VMEM((2 * nb, LANES), F32),
                        pltpu.VMEM((1, s, MOBA_BLOCK), F32), pltpu.VMEM((1, s, MOBA_BLOCK), F32),
                        pltpu.VMEM((1, s, MOBA_BLOCK), BF16), pltpu.VMEM((1, s, MOBA_BLOCK), BF16),
                        pltpu.VMEM((1, s, MOBA_BLOCK), BF16)],
        compiler_params=pltpu.CompilerParams(dimension_semantics=("arbitrary",),
                                             vmem_limit_bytes=VMEM_LIMIT_BYTES),
        name="moba",
    )(q, k, v, *[c[0] for c in casts])
    return outs[0], outs[1:]


def _mix_ffn_kernel(x_ref, yb_ref, gin_ref, wmix_ref, lng_ref, lnb_ref, ws_ref, bmat_ref,
                    wpa_ref, wpb_ref, wo_ref, gmix_ref, gpre_ref, wgu_ref, wd_ref, gffn_ref,
                    *rest):
    next_qkv = len(rest) > 1
    o_ref = rest[4] if next_qkv else rest[0]
    tm = x_ref.shape[0]
    lane = lax.broadcasted_iota(jnp.int32, (1, LANES), 1)
    first = lane < GM_GROUP_DIM
    zero = jnp.zeros((GM_CHUNK, LANES), BF16)
    wi = lax.broadcasted_iota(jnp.int32, (GM_CHUNK, 2 * GM_CHUNK), 0)
    wj = lax.broadcasted_iota(jnp.int32, (GM_CHUNK, 2 * GM_CHUNK), 1) % GM_CHUNK
    tril = wj <= wi
    bmat = bmat_ref[...]
    w_tril = [jnp.where(tril, ws_ref[gp], jnp.zeros_like(ws_ref[gp])) for gp in range(GM_WIDTH // LANES)]
    n_chunks = TOKEN_SUB // GM_CHUNK

    def rows_of(r):
        return slice(r * TOKEN_SUB, (r + 1) * TOKEN_SUB)

    def proj(key, lo, width):
        def stage(r, st):
            if "h0" not in st:
                st["h0"] = _rms(x_ref[rows_of(r), :], gin_ref[...]).astype(BF16)
            st[key] = jnp.dot(st["h0"], wmix_ref[:, lo:lo + width], preferred_element_type=F32)
        return stage

    def gelu_u(r, st):
        st["u"] = jax.nn.gelu(st.pop("zu"))

    def gelu_ln_v(r, st):
        v = jax.nn.gelu(st.pop("zv"))
        mu = jnp.mean(v, axis=-1, keepdims=True)
        vc = v - mu
        var = jnp.mean(vc * vc, axis=-1, keepdims=True)
        st["v"] = (vc * lax.rsqrt(var + NORM_EPS) * lng_ref[...] + lnb_ref[...]).astype(BF16)

    def gating(r, st):
        v = st.pop("v")
        u = st.pop("u")
        cols = []
        for gp in range(GM_WIDTH // LANES):
            stacked = []
            for c in range(n_chunks):
                vp = v[c * GM_CHUNK:(c + 1) * GM_CHUNK, gp * LANES:(gp + 1) * LANES]
                stacked.append(jnp.concatenate([jnp.where(first, vp, zero), jnp.where(first, zero, vp)],
                                               axis=0))
            cols.append(jnp.dot(w_tril[gp], jnp.concatenate(stacked, axis=1), preferred_element_type=F32))
        ya = []
        for c in range(n_chunks):
            mixed = jnp.concatenate([col[:, c * LANES:(c + 1) * LANES] for col in cols], axis=1) + bmat
            ya.append((u[c * GM_CHUNK:(c + 1) * GM_CHUNK] * mixed).astype(BF16))
        st["ya"] = jnp.concatenate(ya, axis=0)

    def branch(key, src, w_ref, gate_key):
        def stage(r, st):
            operand = st.pop("ya") if src == "ya" else yb_ref[rows_of(r), :]
            p = jnp.dot(operand, w_ref[...], preferred_element_type=F32)
            st[key] = jax.nn.sigmoid(st.pop(gate_key)) * p
        return stage

    def project(r, st):
        merged = (st.pop("ma") + st.pop("mb")).astype(BF16)
        st.pop("h0")
        y = jnp.dot(merged, wo_ref[...], preferred_element_type=F32)
        x1 = x_ref[rows_of(r), :] + _rms(y, gmix_ref[...])
        st["x1"] = x1
        st["h"] = _rms(x1, gpre_ref[...]).astype(BF16)

    def up(c):
        lo, width = FFN_CHUNKS[c]

        def stage(r, st):
            st["gate"] = jnp.dot(st["h"], wgu_ref[:, lo:lo + width], preferred_element_type=F32)
            st["lin"] = jnp.dot(st["h"], wgu_ref[:, D_FF + lo:D_FF + lo + width], preferred_element_type=F32)
        return stage

    def down(c):
        lo, width = FFN_CHUNKS[c]

        def stage(r, st):
            act = (jax.nn.silu(st.pop("gate")) * st.pop("lin")).astype(BF16)
            part = jnp.dot(act, wd_ref[lo:lo + width, :], preferred_element_type=F32)
            st["f"] = part if c == 0 else st["f"] + part
        return stage

    def finish(r, st):
        x2 = st.pop("x1") + _rms(st.pop("f"), gffn_ref[...])
        o_ref[rows_of(r), :] = x2
        if next_qkv:
            st["hq"] = _rms(x2, rest[0][...]).astype(BF16)

    stages = [proj("zu", 0, GM_WIDTH), proj("zv", GM_WIDTH, GM_WIDTH), gelu_u, gelu_ln_v,
              proj("zga", 2 * GM_WIDTH, D_MODEL), gating, proj("zgb", 2 * GM_WIDTH + D_MODEL, D_MODEL),
              branch("ma", "ya", wpa_ref, "zga"), branch("mb", "yb", wpb_ref, "zgb"), project]
    for c in range(len(FFN_CHUNKS)):
        stages += [up(c), down(c)]
    stages.append(finish)
    if next_qkv:
        wqkv_ref = rest[1]

        def qkv_section(lo, ep):
            def stage(r, st):
                ep(jnp.dot(st["hq"], wqkv_ref[:, lo:lo + ATTN_WIDTH], preferred_element_type=F32), rows_of(r))
            return stage
        stages += [qkv_section(lo, ep) for lo, ep in _qkv_sections(rest[2], rest[3], *rest[5:8])]

    state = [{} for _ in range(tm // TOKEN_SUB)]
    for stage in stages:
        for r, st in enumerate(state):
            stage(r, st)


def _mix_ffn(x, yb, params, l, tm, casts=(), next_qkv=None):
    t = x.shape[0]
    steps = t // tm
    row = lambda width: pl.BlockSpec((tm, width), lambda i: (i, 0))
    c_in, c_out, c_shape, dests = _cast_plan(casts, steps)
    operands = [x, yb] + [p for p, _ in params]
    in_specs = [row(D_MODEL), row(ATTN_WIDTH)] + [_layer_spec(p, l if stacked else None) for p, stacked in params]
    out_specs = [row(D_MODEL)]
    out_shape = [jax.ShapeDtypeStruct((t, D_MODEL), F32)]
    if next_qkv is not None:
        g_next, w_qkv, cos, sin = next_qkv
        operands += [g_next, w_qkv, cos, sin]
        in_specs += [_layer_spec(g_next, l + 1), _layer_spec(w_qkv, None), row(LANES), row(LANES)]
        out_specs += [row(ATTN_WIDTH)] * 3
        out_shape += [jax.ShapeDtypeStruct((t, ATTN_WIDTH), BF16)] * 3
    n_out = len(out_shape)
    outs = pl.pallas_call(
        _with_casts(_mix_ffn_kernel, len(operands), n_out, dests),
        grid=(steps,),
        in_specs=in_specs + c_in,
        out_specs=out_specs + c_out,
        out_shape=out_shape + c_shape,
        compiler_params=pltpu.CompilerParams(dimension_semantics=("arbitrary",),
                                             vmem_limit_bytes=VMEM_LIMIT_BYTES),
        name="mix_ffn",
    )(*operands, *[c[0] for c in casts])
    return outs[0], (tuple(outs[1:4]) if next_qkv is not None else None), outs[n_out:]


def _rope_tables(positions):
    half = HEAD_DIM // 2
    inv_freq = 1.0 / (ROPE_THETA ** (jnp.arange(0, HEAD_DIM, 2, dtype=F32) / HEAD_DIM))
    ang = positions.astype(F32).reshape(-1, 1) * inv_freq
    cos = jnp.cos(ang)
    sin = jnp.sin(ang)
    cos_t = jnp.tile(cos, (1, LANES // half))
    sin_t = jnp.tile(jnp.concatenate([-sin, sin], axis=1), (1, HEADS_PER_TILE))
    return cos_t, sin_t


def kernel(x, positions, w_in, w_s, b_s, ln_v_g, ln_v_b, w_proj_a, w_proj_b, w_out,
           g_mix_pre, g_mix_post, g_ffn_pre, g_ffn_post, w_gate_up, w_down):
    b, s, d = x.shape
    depth = w_in.shape[0]
    t = b * s
    cos_t, sin_t = _rope_tables(positions)
    xf = x.reshape(t, d)
    gains = lambda g: g.reshape(depth, 1, -1)
    ws = w_s.reshape(depth, GM_GROUPS // 2, 2, GM_CHUNK, GM_CHUNK).transpose(0, 1, 3, 2, 4)
    ws = ws.reshape(depth, GM_GROUPS // 2, GM_CHUNK, 2 * GM_CHUNK).astype(BF16)
    bmat = jnp.repeat(b_s.transpose(0, 2, 1), GM_GROUP_DIM, axis=2)
    LATE_KEYS = ("mix", "gu", "d", "pa", "pb", "o")
    ALL_KEYS = ("mix", "qkv", "gu", "d", "pa", "pb", "o")

    def attn_casts(layer, late_only):
        whole = lambda stack, rep=1: (stack, layer, (None,), rep)
        in_dests = (MIX_COLS,) if late_only else (MIX_COLS, QKV_COLS)
        return [(w_in, layer, in_dests, 1), whole(w_gate_up), whole(w_down, 2), whole(w_proj_a),
                whole(w_proj_b), whole(w_out)]

    weights = {0: {"qkv": w_in[0][:, QKV_LO:GATE_LO].astype(BF16)}}
    qkv, _ = _qkv_proj(xf, gains(g_mix_pre), weights[0]["qkv"], cos_t, sin_t, 0, tm=QKV_TILE)
    for l in range(depth):
        nxt = l + 1 < depth
        casts = (attn_casts(0, True) if l == 0 else []) + (attn_casts(l + 1, False) if nxt else [])
        q, k, va = (a.reshape(b, s, -1) for a in qkv)
        yb, cast_out = _moba(q, k, va, casts)
        cast_out = list(cast_out)
        if l == 0:
            weights[0].update(zip(LATE_KEYS, cast_out[:len(LATE_KEYS)]))
            cast_out = cast_out[len(LATE_KEYS):]
        if nxt:
            weights[l + 1] = dict(zip(ALL_KEYS, cast_out))
        w = weights.pop(l)
        params = [(gains(g_mix_pre), True), (w["mix"], False), (gains(ln_v_g), True), (gains(ln_v_b), True),
                  (ws, True), (bmat, True), (w["pa"], False), (w["pb"], False), (w["o"], False),
                  (gains(g_mix_post), True), (gains(g_ffn_pre), True), (w["gu"], False), (w["d"], False),
                  (gains(g_ffn_post), True)]
        next_qkv = (gains(g_mix_pre), weights[l + 1]["qkv"], cos_t, sin_t) if nxt else None
        xf, qkv, _ = _mix_ffn(xf, yb.reshape(t, -1), params, l, tm=MIX_FFN_TILE, next_qkv=next_qkv)
    return xf.reshape(b, s, d)
```

```python
import jax
import jax.numpy as jnp
from jax import lax
from jax.experimental import pallas as pl
from jax.experimental.pallas import tpu as pltpu

D_MODEL = 1024
GM_GROUPS = 8
GM_GROUP_DIM = 64
GM_WIDTH = GM_GROUPS * GM_GROUP_DIM
GM_CHUNK = 128
ATTN_HEADS = 8
HEAD_DIM = 64
ATTN_WIDTH = ATTN_HEADS * HEAD_DIM
MOBA_BLOCK = 256
MOBA_TOPK = 3
ROPE_THETA = 10000.0
D_FF = 2816
IN_WIDTH = 2 * GM_WIDTH + 3 * ATTN_WIDTH + 2 * D_MODEL
NORM_EPS = 1e-6
NEG_INF = -1e30
LOG2_E = 1.4426950408889634

QKV_LO = 2 * GM_WIDTH
GATE_LO = QKV_LO + 3 * ATTN_WIDTH
MIX_COLS = ((0, QKV_LO), (GATE_LO, IN_WIDTH))
QKV_COLS = ((QKV_LO, GATE_LO),)

LANES = 128
HEADS_PER_TILE = LANES // HEAD_DIM
VMEM_LIMIT_BYTES = 56 * 1024 * 1024
MXU_TILE = 256

MIX_FFN_TILE = 512
QKV_TILE = 1024

TOKEN_SUB = 256
FFN_CHUNKS = ((0, 5 * MXU_TILE), (5 * MXU_TILE, D_FF - 5 * MXU_TILE))

F32 = jnp.float32
BF16 = jnp.bfloat16


def _rms(x, g):
    return x * lax.rsqrt(jnp.mean(x * x, axis=-1, keepdims=True) + NORM_EPS) * g


def _layer_spec(param, l):
    if l is None:
        return pl.BlockSpec(param.shape, lambda *_: (0,) * param.ndim, pipeline_mode=pl.Buffered(1))
    shape = param.shape[1:]
    return pl.BlockSpec((None,) + shape, lambda *_: (l,) + (0,) * len(shape),
                        pipeline_mode=pl.Buffered(1))


def _cast_plan(items, steps):
    in_specs, out_specs, out_shapes, dests_per_item = [], [], [], []
    for w, layer, dests, rep in items:
        _, rows, cols = w.shape
        rb = rows * rep // steps
        in_specs.append(pl.BlockSpec((None, rb, cols), lambda i, layer=layer, rep=rep: (layer, i // rep, 0)))
        dests = tuple(((0, cols),) if d is None else d for d in dests)
        for d in dests:
            width = sum(hi - lo for lo, hi in d)
            out_specs.append(pl.BlockSpec((rb, width), lambda i, rep=rep: (i // rep, 0)))
            out_shapes.append(jax.ShapeDtypeStruct((rows, width), BF16))
        dests_per_item.append(dests)
    return in_specs, out_specs, out_shapes, dests_per_item


def _with_casts(body, n_in, n_out, dests_per_item):
    n_src = len(dests_per_item)
    n_dst = sum(len(d) for d in dests_per_item)

    def kernel_fn(*refs):
        ins, rest = refs[:n_in], refs[n_in:]
        srcs, rest = rest[:n_src], rest[n_src:]
        outs, rest = rest[:n_out], rest[n_out:]
        dsts, scratch = list(rest[:n_dst]), rest[n_dst:]
        for src, dests in zip(srcs, dests_per_item):
            for ranges in dests:
                dst = dsts.pop(0)
                off = 0
                for lo, hi in ranges:
                    dst[:, off:off + hi - lo] = src[:, lo:hi].astype(BF16)
                    off += hi - lo
        body(*ins, *outs, *scratch)
    return kernel_fn


def _qkv_sections(cos_ref, sin_ref, q_ref, k_ref, va_ref):
    first_half = (lax.broadcasted_iota(jnp.int32, (TOKEN_SUB, LANES), 1) % HEAD_DIM) < HEAD_DIM // 2

    def rope_out(out_ref, scale):
        def ep(z, rows):
            cos = cos_ref[rows, :]
            sin = sin_ref[rows, :]
            for c in range(ATTN_WIDTH // LANES):
                zc = z[:, c * LANES:(c + 1) * LANES]
                partner = jnp.where(first_half,
                                    pltpu.roll(zc, LANES - HEAD_DIM // 2, 1),
                                    pltpu.roll(zc, HEAD_DIM // 2, 1))
                r = zc * cos + partner * sin
                if scale != 1.0:
                    r = r * scale
                out_ref[rows, c * LANES:(c + 1) * LANES] = r.astype(BF16)
        return ep

    def cast_out(z, rows):
        va_ref[rows, :] = z.astype(BF16)

    return [
        (0, rope_out(q_ref, HEAD_DIM ** -0.5 * LOG2_E)),
        (ATTN_WIDTH, rope_out(k_ref, 1.0)),
        (2 * ATTN_WIDTH, cast_out),
    ]


def _qkv_proj_kernel(x_ref, g_ref, w_ref, cos_ref, sin_ref, q_ref, k_ref, va_ref):
    tm = x_ref.shape[0]
    sections = _qkv_sections(cos_ref, sin_ref, q_ref, k_ref, va_ref)
    items = [(r, sec) for r in range(tm // TOKEN_SUB) for sec in sections]

    h = {}
    pending = None
    for r, (lo, ep) in items:
        rows = slice(r * TOKEN_SUB, (r + 1) * TOKEN_SUB)
        if r not in h:
            h[r] = _rms(x_ref[rows, :], g_ref[...]).astype(BF16)
        z = jnp.dot(h[r], w_ref[:, lo:lo + ATTN_WIDTH], preferred_element_type=F32)
        if pending is not None:
            pending[0](pending[1], pending[2])
        pending = (ep, z, rows)
    pending[0](pending[1], pending[2])


def _qkv_proj(x, g, w_qkv, cos, sin, l, tm, casts=()):
    t = x.shape[0]
    steps = t // tm
    row = lambda width: pl.BlockSpec((tm, width), lambda i: (i, 0))
    c_in, c_out, c_shape, dests = _cast_plan(casts, steps)
    outs = pl.pallas_call(
        _with_casts(_qkv_proj_kernel, 5, 3, dests),
        grid=(steps,),
        in_specs=[row(D_MODEL), _layer_spec(g, l), _layer_spec(w_qkv, None), row(LANES), row(LANES)] + c_in,
        out_specs=[row(ATTN_WIDTH)] * 3 + c_out,
        out_shape=[jax.ShapeDtypeStruct((t, ATTN_WIDTH), BF16)] * 3 + c_shape,
        compiler_params=pltpu.CompilerParams(dimension_semantics=("arbitrary",),
                                             vmem_limit_bytes=VMEM_LIMIT_BYTES),
        name="qkv_proj",
    )(x, g, w_qkv, cos, sin, *[c[0] for c in casts])
    return outs[:3], outs[3:]


def _moba_kernel(q_ref, k_ref, v_ref, o_ref, vt_ref, kbar_ref, s0_ref, s1_ref, p0_ref, p1_ref, p2_ref, p3_ref):
    nb = k_ref.shape[1] // MOBA_BLOCK
    lane = lax.broadcasted_iota(jnp.int32, (1, LANES), 1)
    key_id = lax.broadcasted_iota(jnp.int32, (MOBA_BLOCK, MOBA_BLOCK), 0)
    qry_id = lax.broadcasted_iota(jnp.int32, (MOBA_BLOCK, MOBA_BLOCK), 1)
    causal = key_id <= qry_id
    jidx = lax.broadcasted_iota(jnp.int32, (nb, MOBA_BLOCK), 0)
    drow = lax.broadcasted_iota(jnp.int32, (LANES, 1), 0)
    nt = (((1,), (1,)), ((), ()))

    vt = v_ref[0].astype(F32).T
    for hh in range(HEADS_PER_TILE):
        vt_ref[hh] = jnp.where((drow // HEAD_DIM) == hh, vt, 1.0).astype(BF16)
    for j in range(nb):
        kb = k_ref[0, j * MOBA_BLOCK:(j + 1) * MOBA_BLOCK, :]
        kbar = jnp.mean(kb.astype(F32), axis=0, keepdims=True)
        hi = kbar.astype(BF16)
        lo = (kbar - hi.astype(F32)).astype(BF16)
        kbar_ref[j:j + 1, :] = hi.astype(F32)
        kbar_ref[nb + j:nb + j + 1, :] = lo.astype(F32)
    kbar2 = kbar_ref[...].astype(BF16)

    def scores(n, hh):
        past = n * MOBA_BLOCK
        q = q_ref[0, past:past + MOBA_BLOCK, :]
        own = (lane // HEAD_DIM) == hh
        q_own = jnp.where(own, q, jnp.zeros_like(q))
        if n > MOBA_TOPK:
            g2 = lax.dot_general(kbar2, q_own, nt, preferred_element_type=F32)
            gate = jnp.where(jidx < n, g2[:nb] + g2[nb:], -jnp.inf)
            cnt = jnp.zeros((nb, MOBA_BLOCK), jnp.int32)
            for jp in range(n):
                gj = gate[jp:jp + 1, :]
                ahead = (gj > gate) | ((gj == gate) & (jp < jidx))
                cnt = cnt + jnp.where(ahead, 1, 0)
            bias = jnp.where(cnt < MOBA_TOPK, 0.0, NEG_INF)
        else:
            bias = None
        return {"n": n, "hh": hh, "q": q_own, "bias": bias, "m": None}

    def block(j):
        return slice(j * MOBA_BLOCK, (j + 1) * MOBA_BLOCK)

    s_refs, p_refs = (s0_ref, s1_ref), (p0_ref, p1_ref, p2_ref, p3_ref)
    dyn0 = lax.shift_right_logical(pl.program_id(0), 30)

    def score_block(it, slot, j):
        n, bias = it["n"], it["bias"]
        sj = lax.dot_general(k_ref[0, block(j), :], it["q"], nt, preferred_element_type=F32)
        if j == n:
            sj = jnp.where(causal, sj, NEG_INF)
        s_refs[slot][dyn0, block(j), :] = sj
        mj = jnp.max(sj, axis=0, keepdims=True)
        if bias is not None and j < n:
            mj = mj + bias[j:j + 1, :]
        it["m"] = mj if it["m"] is None else jnp.maximum(it["m"], mj)

    def prob_block(it, slot, pslot, j):
        n, bias, m = it["n"], it["bias"], it["m"]
        shift = m if (bias is None or j == n) else m - bias[j:j + 1, :]
        p_refs[pslot][dyn0, block(j), :] = jnp.exp2(s_refs[slot][dyn0, block(j), :] - shift).astype(BF16)

    def output(it, slot):
        keys = (it["n"] + 1) * MOBA_BLOCK
        acc = jnp.dot(vt_ref[it["hh"], :, :keys], p_refs[slot][dyn0, :keys, :], preferred_element_type=F32)
        ones_row = HEAD_DIM * (1 - it["hh"])
        return acc / acc[ones_row:ones_row + 1, :]

    order = [(n, hh) for n in reversed(range(nb)) for hh in range(HEADS_PER_TILE)]
    items, outs = {}, {}
    for i in range(len(order) + 4):
        if i < len(order):
            items[i] = scores(*order[i])
        for j in range(nb):
            if i in items and j <= items[i]["n"]:
                score_block(items[i], i % 2, j)
            if i - 1 in items and j <= items[i - 1]["n"]:
                prob_block(items[i - 1], (i - 1) % 2, (i - 1) % 4, j)
        if i - 4 in items:
            it = items.pop(i - 4)
            outs[it["hh"]] = output(it, (i - 4) % 4)
            if it["hh"] == HEADS_PER_TILE - 1:
                out_t = jnp.where(drow < HEAD_DIM, outs[0], outs[1])
                o_ref[0, block(it["n"]), :] = out_t.T.astype(BF16)


def _moba(q, k, v, casts=()):
    b, s, w = q.shape
    tiles = w // LANES
    nb = s // MOBA_BLOCK
    steps = b * tiles
    spec = pl.BlockSpec((1, s, LANES), lambda bp: (bp // tiles, 0, bp % tiles))
    c_in, c_out, c_shape, dests = _cast_plan(casts, steps)
    outs = pl.pallas_call(
        _with_casts(_moba_kernel, 3, 1, dests),
        grid=(steps,),
        in_specs=[spec, spec, spec] + c_in,
        out_specs=[spec] + c_out,
        out_shape=[jax.ShapeDtypeStruct((b, s, w), BF16)] + c_shape,
        scratch_shapes=[pltpu.VMEM((HEADS_PER_TILE, LANES, s), BF16),
                        pltpu.VMEM((2 * nb, LANES), F32),
                        pltpu.VMEM((1, s, MOBA_BLOCK), F32), pltpu.VMEM((1, s, MOBA_BLOCK), F32),
                        pltpu.VMEM((1, s, MOBA_BLOCK), BF16), pltpu.VMEM((1, s, MOBA_BLOCK), BF16),
                        pltpu.VMEM((1, s, MOBA_BLOCK), BF16), pltpu.VMEM((1, s, MOBA_BLOCK), BF16)],
        compiler_params=pltpu.CompilerParams(dimension_semantics=("arbitrary",),
                                             vmem_limit_bytes=VMEM_LIMIT_BYTES),
        name="moba",
    )(q, k, v, *[c[0] for c in casts])
    return outs[0], outs[1:]


def _mix_ffn_kernel(x_ref, yb_ref, gin_ref, wmix_ref, lng_ref, lnb_ref, ws_ref, bmat_ref,
                    wpa_ref, wpb_ref, wo_ref, gmix_ref, gpre_ref, wgu_ref, wd_ref, gffn_ref,
                    *rest):
    next_qkv = len(rest) > 1
    o_ref = rest[4] if next_qkv else rest[0]
    tm = x_ref.shape[0]
    lane = lax.broadcasted_iota(jnp.int32, (1, LANES), 1)
    first = lane < GM_GROUP_DIM
    zero = jnp.zeros((GM_CHUNK, LANES), BF16)
    wi = lax.broadcasted_iota(jnp.int32, (GM_CHUNK, 2 * GM_CHUNK), 0)
    wj = lax.broadcasted_iota(jnp.int32, (GM_CHUNK, 2 * GM_CHUNK), 1) % GM_CHUNK
    tril = wj <= wi
    bmat = bmat_ref[...]
    w_tril = [jnp.where(tril, ws_ref[gp], jnp.zeros_like(ws_ref[gp])) for gp in range(GM_WIDTH // LANES)]
    n_chunks = TOKEN_SUB // GM_CHUNK

    def rows_of(r):
        return slice(r * TOKEN_SUB, (r + 1) * TOKEN_SUB)

    def proj(key, lo, width):
        def stage(r, st):
            if "h0" not in st:
                st["h0"] = _rms(x_ref[rows_of(r), :], gin_ref[...]).astype(BF16)
            st[key] = jnp.dot(st["h0"], wmix_ref[:, lo:lo + width], preferred_element_type=F32)
        return stage

    def gelu_u(r, st):
        st["u"] = jax.nn.gelu(st.pop("zu"))

    def gelu_ln_v(r, st):
        v = jax.nn.gelu(st.pop("zv"))
        mu = jnp.mean(v, axis=-1, keepdims=True)
        vc = v - mu
        var = jnp.mean(vc * vc, axis=-1, keepdims=True)
        st["v"] = (vc * lax.rsqrt(var + NORM_EPS) * lng_ref[...] + lnb_ref[...]).astype(BF16)

    def gating(r, st):
        v = st.pop("v")
        u = st.pop("u")
        cols = []
        for gp in range(GM_WIDTH // LANES):
            stacked = []
            for c in range(n_chunks):
                vp = v[c * GM_CHUNK:(c + 1) * GM_CHUNK, gp * LANES:(gp + 1) * LANES]
                stacked.append(jnp.concatenate([jnp.where(first, vp, zero), jnp.where(first, zero, vp)],
                                               axis=0))
            cols.append(jnp.dot(w_tril[gp], jnp.concatenate(stacked, axis=1), preferred_element_type=F32))
        ya = []
        for c in range(n_chunks):
            mixed = jnp.concatenate([col[:, c * LANES:(c + 1) * LANES] for col in cols], axis=1) + bmat
            ya.append((u[c * GM_CHUNK:(c + 1) * GM_CHUNK] * mixed).astype(BF16))
        st["ya"] = jnp.concatenate(ya, axis=0)

    def branch(key, src, w_ref, gate_key):
        def stage(r, st):
            operand = st.pop("ya") if src == "ya" else yb_ref[rows_of(r), :]
            p = jnp.dot(operand, w_ref[...], preferred_element_type=F32)
            st[key] = jax.nn.sigmoid(st.pop(gate_key)) * p
        return stage

    def project(r, st):
        merged = (st.pop("ma") + st.pop("mb")).astype(BF16)
        st.pop("h0")
        y = jnp.dot(merged, wo_ref[...], preferred_element_type=F32)
        x1 = x_ref[rows_of(r), :] + _rms(y, gmix_ref[...])
        st["x1"] = x1
        st["h"] = _rms(x1, gpre_ref[...]).astype(BF16)

    def up(c):
        lo, width = FFN_CHUNKS[c]

        def stage(r, st):
            st["gate"] = jnp.dot(st["h"], wgu_ref[:, lo:lo + width], preferred_element_type=F32)
            st["lin"] = jnp.dot(st["h"], wgu_ref[:, D_FF + lo:D_FF + lo + width], preferred_element_type=F32)
        return stage

    def down(c):
        lo, width = FFN_CHUNKS[c]

        def stage(r, st):
            act = (jax.nn.silu(st.pop("gate")) * st.pop("lin")).astype(BF16)
            part = jnp.dot(act, wd_ref[lo:lo + width, :], preferred_element_type=F32)
            st["f"] = part if c == 0 else st["f"] + part
        return stage

    def finish(r, st):
        x2 = st.pop("x1") + _rms(st.pop("f"), gffn_ref[...])
        o_ref[rows_of(r), :] = x2
        if next_qkv:
            st["hq"] = _rms(x2, rest[0][...]).astype(BF16)

    stages = [proj("zu", 0, GM_WIDTH), proj("zv", GM_WIDTH, GM_WIDTH), gelu_u, gelu_ln_v,
              proj("zga", 2 * GM_WIDTH, D_MODEL), gating, proj("zgb", 2 * GM_WIDTH + D_MODEL, D_MODEL),
              branch("ma", "ya", wpa_ref, "zga"), branch("mb", "yb", wpb_ref, "zgb"), project]
    for c in range(len(FFN_CHUNKS)):
        stages += [up(c), down(c)]
    stages.append(finish)
    if next_qkv:
        wqkv_ref = rest[1]

        def qkv_section(lo, ep):
            def stage(r, st):
                ep(jnp.dot(st["hq"], wqkv_ref[:, lo:lo + ATTN_WIDTH], preferred_element_type=F32), rows_of(r))
            return stage
        stages += [qkv_section(lo, ep) for lo, ep in _qkv_sections(rest[2], rest[3], *rest[5:8])]

    state = [{} for _ in range(tm // TOKEN_SUB)]
    for stage in stages:
        for r, st in enumerate(state):
            stage(r, st)


def _mix_ffn(x, yb, params, l, tm, casts=(), next_qkv=None):
    t = x.shape[0]
    steps = t // tm
    row = lambda width: pl.BlockSpec((tm, width), lambda i: (i, 0))
    c_in, c_out, c_shape, dests = _cast_plan(casts, steps)
    operands = [x, yb] + [p for p, _ in params]
    in_specs = [row(D_MODEL), row(ATTN_WIDTH)] + [_layer_spec(p, l if stacked else None) for p, stacked in params]
    out_specs = [row(D_MODEL)]
    out_shape = [jax.ShapeDtypeStruct((t, D_MODEL), F32)]
    if next_qkv is not None:
        g_next, w_qkv, cos, sin = next_qkv
        operands += [g_next, w_qkv, cos, sin]
        in_specs += [_layer_spec(g_next, l + 1), _layer_spec(w_qkv, None), row(LANES), row(LANES)]
        out_specs += [row(ATTN_WIDTH)] * 3
        out_shape += [jax.ShapeDtypeStruct((t, ATTN_WIDTH), BF16)] * 3
    n_out = len(out_shape)
    outs = pl.pallas_call(
        _with_casts(_mix_ffn_kernel, len(operands), n_out, dests),
        grid=(steps,),
        in_specs=in_specs + c_in,
        out_specs=out_specs + c_out,
        out_shape=out_shape + c_shape,
        compiler_params=pltpu.CompilerParams(dimension_semantics=("arbitrary",),
                                             vmem_limit_bytes=VMEM_LIMIT_BYTES),
        name="mix_ffn",
    )(*operands, *[c[0] for c in casts])
    return outs[0], (tuple(outs[1:4]) if next_qkv is not None else None), outs[n_out:]


def _rope_tables(positions):
    half = HEAD_DIM // 2
    inv_freq = 1.0 / (ROPE_THETA ** (jnp.arange(0, HEAD_DIM, 2, dtype=F32) / HEAD_DIM))
    ang = positions.astype(F32).reshape(-1, 1) * inv_freq
    cos = jnp.cos(ang)
    sin = jnp.sin(ang)
    cos_t = jnp.tile(cos, (1, LANES // half))
    sin_t = jnp.tile(jnp.concatenate([-sin, sin], axis=1), (1, HEADS_PER_TILE))
    return cos_t, sin_t


def kernel(x, positions, w_in, w_s, b_s, ln_v_g, ln_v_b, w_proj_a, w_proj_b, w_out,
           g_mix_pre, g_mix_post, g_ffn_pre, g_ffn_post, w_gate_up, w_down):
    b, s, d = x.shape
    depth = w_in.shape[0]
    t = b * s
    cos_t, sin_t = _rope_tables(positions)
    xf = x.reshape(t, d)
    gains = lambda g: g.reshape(depth, 1, -1)
    ws = w_s.reshape(depth, GM_GROUPS // 2, 2, GM_CHUNK, GM_CHUNK).transpose(0, 1, 3, 2, 4)
    ws = ws.reshape(depth, GM_GROUPS // 2, GM_CHUNK, 2 * GM_CHUNK).astype(BF16)
    bmat = jnp.repeat(b_s.transpose(0, 2, 1), GM_GROUP_DIM, axis=2)
    LATE_KEYS = ("mix", "gu", "d", "pa", "pb", "o")
    ALL_KEYS = ("mix", "qkv", "gu", "d", "pa", "pb", "o")

    def attn_casts(layer, late_only):
        whole = lambda stack, rep=1: (stack, layer, (None,), rep)
        in_dests = (MIX_COLS,) if late_only else (MIX_COLS, QKV_COLS)
        return [(w_in, layer, in_dests, 1), whole(w_gate_up), whole(w_down, 2), whole(w_proj_a),
                whole(w_proj_b), whole(w_out)]

    weights = {0: {"qkv": w_in[0][:, QKV_LO:GATE_LO].astype(BF16)}}
    qkv, _ = _qkv_proj(xf, gains(g_mix_pre), weights[0]["qkv"], cos_t, sin_t, 0, tm=QKV_TILE)
    for l in range(depth):
        nxt = l + 1 < depth
        casts = (attn_casts(0, True) if l == 0 else []) + (attn_casts(l + 1, False) if nxt else [])
        q, k, va = (a.reshape(b, s, -1) for a in qkv)
        yb, cast_out = _moba(q, k, va, casts)
        cast_out = list(cast_out)
        if l == 0:
            weights[0].update(zip(LATE_KEYS, cast_out[:len(LATE_KEYS)]))
            cast_out = cast_out[len(LATE_KEYS):]
        if nxt:
            weights[l + 1] = dict(zip(ALL_KEYS, cast_out))
        w = weights.pop(l)
        params = [(gains(g_mix_pre), True), (w["mix"], False), (gains(ln_v_g), True), (gains(ln_v_b), True),
                  (ws, True), (bmat, True), (w["pa"], False), (w["pb"], False), (w["o"], False),
                  (gains(g_mix_post), True), (gains(g_ffn_pre), True), (w["gu"], False), (w["d"], False),
                  (gains(g_ffn_post), True)]
        next_qkv = (gains(g_mix_pre), weights[l + 1]["qkv"], cos_t, sin_t) if nxt else None
        xf, qkv, _ = _mix_ffn(xf, yb.reshape(t, -1), params, l, tm=MIX_FFN_TILE, next_qkv=next_qkv)
    return xf.reshape(b, s, d)
```
